```python
import math
import jax, jax.numpy as jnp
from jax import lax
import numpy as np

D_MODEL = 1024
BATCH = 8
SEQ = 2048
DEPTH = 1
DEC_BATCH = 8
DEC_SEQ = 32
PAST_LEN = 2048

CHUNK = 64
CONV_W = 4
EPS = 1e-6
DN_HEADS = 8
DN_DK = 128
DN_DV = 128
DN_QK = DN_HEADS * DN_DK
DN_V = DN_HEADS * DN_DV
DN_CONV_CH = 2 * DN_QK + DN_V
LRU_WIDTH = D_MODEL
LRU_BLOCKS = 8
LRU_BDIM = LRU_WIDTH // LRU_BLOCKS
LRU_C = 8.0
PEER_HEADS = 8
PEER_NKEYS = 128
PEER_N = PEER_NKEYS * PEER_NKEYS
PEER_DK = 128
PEER_DKH = PEER_DK // 2
PEER_TOPK = 16
PEER_TOKEN_BLOCK = 128
IN_SPLITS = (DN_CONV_CH, DN_V, LRU_WIDTH, LRU_WIDTH, D_MODEL, D_MODEL, DN_HEADS, DN_HEADS)
IN_COLS = DN_CONV_CH + DN_V + 2 * LRU_WIDTH + 2 * D_MODEL + 2 * DN_HEADS

kernel_name = "hybrid_gdn_rglru_peer_stream_step"


def rmsnorm(x, g):
    xf = x.astype(jnp.float32)
    r = lax.rsqrt(jnp.mean(xf * xf, axis=-1, keepdims=True) + EPS)
    return (xf * r).astype(x.dtype) * g


def l2norm(x):
    return x * lax.rsqrt(jnp.sum(x * x, axis=-1, keepdims=True) + EPS)


def split_cols(t, sizes):
    offs = [int(o) for o in np.cumsum(sizes)[:-1]]
    return jnp.split(t, offs, axis=-1)


def causal_conv(x, buf, w, b=None):
    T = x.shape[1]
    xp = jnp.concatenate([buf.astype(x.dtype), x], axis=1)
    y = xp[:, 0:T] * w[0]
    for j in range(1, CONV_W):
        y = y + xp[:, j:j + T] * w[j]
    if b is not None:
        y = y + b
    return y, xp[:, -(CONV_W - 1):]


def gated_delta_rule(q, k, v, g, beta, S0):
    B, T, H, DK = q.shape
    L = min(CHUNK, T)
    assert T % L == 0
    nc = T // L
    def chunks(t):
        return t.reshape(B, nc, L, H, t.shape[-1]).transpose(1, 0, 3, 2, 4)
    qc, kc, vc = chunks(q), chunks(k), chunks(v)
    gc = g.reshape(B, nc, L, H).transpose(1, 0, 3, 2)
    bc = beta.reshape(B, nc, L, H).transpose(1, 0, 3, 2)
    G = jnp.cumsum(gc, axis=-1)
    incl = jnp.tril(jnp.ones((L, L), dtype=bool))
    strict = jnp.tril(jnp.ones((L, L), dtype=bool), -1)
    diff = G[..., :, None] - G[..., None, :]
    dec = jnp.where(incl, jnp.exp(jnp.where(incl, diff, 0.0)), 0.0)
    kk = jnp.einsum('cbhid,cbhjd->cbhij', kc, kc)
    A = jnp.where(strict, bc[..., :, None] * kk * dec, 0.0) + jnp.eye(L, dtype=q.dtype)
    gam = jnp.exp(G)
    rhs = jnp.concatenate([(bc * gam)[..., None] * kc, bc[..., None] * vc], axis=-1)
    sol = lax.linalg.triangular_solve(A, rhs, left_side=True, lower=True, unit_diagonal=True)
    Wm, Uv = sol[..., :DK], sol[..., DK:]
    QK = jnp.einsum('cbhid,cbhjd->cbhij', qc, kc) * dec
    qg = qc * gam[..., None]
    kend = kc * jnp.exp(G[..., -1:] - G)[..., None]
    gend = gam[..., -1]

    def step(S, inp):
        Wc, Uvc, QKc, qgc, kendc, gendc = inp
        U = Uvc - jnp.einsum('bhlk,bhkv->bhlv', Wc, S)
        o = jnp.einsum('bhlk,bhkv->bhlv', qgc, S) + jnp.einsum('bhij,bhjv->bhiv', QKc, U)
        S = gendc[..., None, None] * S + jnp.einsum('bhlk,bhlv->bhkv', kendc, U)
        return S, o

    S, o = lax.scan(step, S0, (Wm, Uv, QK, qg, kend, gend))
    o = o.transpose(1, 0, 3, 2, 4).reshape(B, T, H, vc.shape[-1])
    return o, S


def rg_lru(x, h0, w_r, b_r, w_i, b_i, lam):
    B, T, _ = x.shape
    xb = x.reshape(B, T, LRU_BLOCKS, LRU_BDIM)
    r = jax.nn.sigmoid(jnp.einsum('btnd,nde->btne', xb, w_r) + b_r).reshape(B, T, LRU_WIDTH)
    i = jax.nn.sigmoid(jnp.einsum('btnd,nde->btne', xb, w_i) + b_i).reshape(B, T, LRU_WIDTH)
    log_a = -LRU_C * r * jax.nn.softplus(-lam)
    a = jnp.exp(log_a)
    bterm = jnp.sqrt(-jnp.expm1(2.0 * log_a)) * (i * x)

    def comb(lhs, rhs):
        a1, b1 = lhs
        a2, b2 = rhs
        return a1 * a2, a2 * b1 + b2

    Acum, Bcum = lax.associative_scan(comb, (a, bterm), axis=1)
    h = Acum * h0[:, None, :] + Bcum
    return h, h[:, -1]


def peer(n, w_q, keys, u_tab, v_tab):
    B, T, D = n.shape
    N = B * T
    nb = -(-N // PEER_TOKEN_BLOCK)
    flat = jnp.pad(n.reshape(N, D), ((0, nb * PEER_TOKEN_BLOCK - N), (0, 0)))
    flat = flat.reshape(nb, PEER_TOKEN_BLOCK, D)

    def block(xb):
        m = xb.shape[0]
        q = (xb @ w_q).reshape(m, PEER_HEADS, 2, PEER_DKH)
        s = jnp.einsum('nhpc,hpkc->nhpk', q, keys).astype(jnp.float32)
        s1, i1 = lax.top_k(s[:, :, 0], PEER_TOPK)
        s2, i2 = lax.top_k(s[:, :, 1], PEER_TOPK)
        cand = (s1[..., :, None] + s2[..., None, :]).reshape(m, PEER_HEADS, PEER_TOPK * PEER_TOPK)
        top, pos = lax.top_k(cand, PEER_TOPK)
        e = (jnp.take_along_axis(i1, pos // PEER_TOPK, axis=-1) * PEER_NKEYS
             + jnp.take_along_axis(i2, pos % PEER_TOPK, axis=-1))
        gate = jax.nn.softmax(top, axis=-1).astype(xb.dtype)
        e = e.reshape(m, PEER_HEADS * PEER_TOPK)
        gate = gate.reshape(m, PEER_HEADS * PEER_TOPK)
        u = jnp.take(u_tab, e, axis=0)
        act = jax.nn.gelu(jnp.einsum('nkd,nd->nk', u, xb), approximate=False) * gate
        vv = jnp.take(v_tab, e, axis=0)
        return jnp.einsum('nk,nkd->nd', act, vv)

    out = lax.map(block, flat).reshape(nb * PEER_TOKEN_BLOCK, D)[:N]
    return out.reshape(B, T, D)


def layer(x, S0, dn_buf0, h0, lru_buf0, w_in, dn_conv_w, dn_a_log, dn_dt_bias, dn_norm_g,
          lru_conv_w, lru_conv_b, lru_w_r, lru_b_r, lru_w_i, lru_b_i, lru_lambda,
          w_branch_a, w_branch_b, w_out, ln1_g, ln2_g, peer_w_q, peer_keys, peer_u, peer_v):
    f32 = jnp.float32
    B, T, _ = x.shape
    n = rmsnorm(x, ln1_g)
    proj = n @ w_in
    qkv, z, lx, ly, ga, gb, bt, dc = split_cols(proj, IN_SPLITS)
    qkv_c, dn_buf = causal_conv(qkv, dn_buf0, dn_conv_w)
    qkv_c = jax.nn.silu(qkv_c).astype(f32)
    q, k, v = jnp.split(qkv_c, [DN_QK, 2 * DN_QK], axis=-1)
    q = l2norm(q.reshape(B, T, DN_HEADS, DN_DK)) * (DN_DK ** -0.5)
    k = l2norm(k.reshape(B, T, DN_HEADS, DN_DK))
    v = v.reshape(B, T, DN_HEADS, DN_DV)
    beta = jax.nn.sigmoid(bt.astype(f32))
    g = -jnp.exp(dn_a_log.astype(f32)) * jax.nn.softplus(dc.astype(f32) + dn_dt_bias.astype(f32))
    o, S = gated_delta_rule(q, k, v, g, beta, S0.astype(f32))
    o = rmsnorm(o, dn_norm_g.astype(f32)) * jax.nn.silu(z.astype(f32).reshape(B, T, DN_HEADS, DN_DV))
    o_a = o.reshape(B, T, DN_V).astype(x.dtype)
    lc, lru_buf = causal_conv(lx, lru_buf0, lru_conv_w, lru_conv_b)
    h, hT = rg_lru(lc.astype(f32), h0.astype(f32), lru_w_r.astype(f32), lru_b_r.astype(f32),
                   lru_w_i.astype(f32), lru_b_i.astype(f32), lru_lambda.astype(f32))
    o_b = h.astype(x.dtype) * jax.nn.gelu(ly)
    mix = jax.nn.sigmoid(ga) * (o_a @ w_branch_a) + jax.nn.sigmoid(gb) * (o_b @ w_branch_b)
    x = x + mix @ w_out
    x = x + peer(rmsnorm(x, ln2_g), peer_w_q, peer_keys, peer_u, peer_v)
    return x, S.astype(x.dtype), dn_buf, hT.astype(x.dtype), lru_buf


def setup_inputs(seed: int = 0) -> dict:
    key = jax.random.key(seed)
    ks = jax.random.split(key, 32)
    f32 = jnp.float32

    def nrm(k, shape, s):
        return jax.random.normal(k, shape, f32) * s

    a0 = jax.random.uniform(ks[17], (DEPTH, LRU_WIDTH), f32, 0.9, 0.999) ** (1.0 / LRU_C)
    dt = jnp.exp(jax.random.uniform(ks[9], (DEPTH, DN_HEADS), f32, math.log(1e-3), math.log(1e-1)))
    return {
        'x_prompt': nrm(ks[0], (BATCH, SEQ, D_MODEL), 1.0),
        'x_sample': nrm(ks[1], (DEC_BATCH, DEC_SEQ, D_MODEL), 1.0),
        'state_dn': nrm(ks[2], (DEPTH, DEC_BATCH, DN_HEADS, DN_DK, DN_DV), 0.1),
        'state_dn_conv': nrm(ks[3], (DEPTH, DEC_BATCH, CONV_W - 1, DN_CONV_CH), 1.0),
        'state_lru_h': nrm(ks[4], (DEPTH, DEC_BATCH, LRU_WIDTH), 0.5),
        'state_lru_conv': nrm(ks[5], (DEPTH, DEC_BATCH, CONV_W - 1, LRU_WIDTH), 1.0),
        'w_in': nrm(ks[6], (DEPTH, D_MODEL, IN_COLS), D_MODEL ** -0.5),
        'dn_conv_w': nrm(ks[7], (DEPTH, CONV_W, DN_CONV_CH), CONV_W ** -0.5),
        'dn_a_log': jnp.log(jax.random.uniform(ks[8], (DEPTH, DN_HEADS), f32, 1.0, 16.0)),
        'dn_dt_bias': dt + jnp.log(-jnp.expm1(-dt)),
        'dn_norm_g': 1.0 + nrm(ks[10], (DEPTH, DN_DV), 0.02),
        'lru_conv_w': nrm(ks[11], (DEPTH, CONV_W, LRU_WIDTH), CONV_W ** -0.5),
        'lru_conv_b': nrm(ks[12], (DEPTH, LRU_WIDTH), 0.02),
        'lru_w_r': nrm(ks[13], (DEPTH, LRU_BLOCKS, LRU_BDIM, LRU_BDIM), LRU_BDIM ** -0.5),
        'lru_b_r': nrm(ks[14], (DEPTH, LRU_BLOCKS, LRU_BDIM), 0.02),
        'lru_w_i': nrm(ks[15], (DEPTH, LRU_BLOCKS, LRU_BDIM, LRU_BDIM), LRU_BDIM ** -0.5),
        'lru_b_i': nrm(ks[16], (DEPTH, LRU_BLOCKS, LRU_BDIM), 0.02),
        'lru_lambda': jnp.log(a0) - jnp.log1p(-a0),
        'w_branch_a': nrm(ks[18], (DEPTH, DN_V, D_MODEL), DN_V ** -0.5),
        'w_branch_b': nrm(ks[19], (DEPTH, LRU_WIDTH, D_MODEL), LRU_WIDTH ** -0.5),
        'w_out': nrm(ks[20], (DEPTH, D_MODEL, D_MODEL), D_MODEL ** -0.5),
        'ln1_g': 1.0 + nrm(ks[21], (DEPTH, D_MODEL), 0.02),
        'ln2_g': 1.0 + nrm(ks[22], (DEPTH, D_MODEL), 0.02),
        'peer_w_q': nrm(ks[23], (DEPTH, D_MODEL, PEER_HEADS * PEER_DK), D_MODEL ** -0.5),
        'peer_keys': nrm(ks[24], (DEPTH, PEER_HEADS, 2, PEER_NKEYS, PEER_DKH), PEER_DKH ** -0.5),
        'peer_u': nrm(ks[25], (DEPTH, PEER_N, D_MODEL), D_MODEL ** -0.5),
        'peer_v': nrm(ks[26], (DEPTH, PEER_N, D_MODEL), 0.5),
        'final_g': 1.0 + nrm(ks[27], (D_MODEL,), 0.02),
    }


def reference(x_prompt, x_sample, state_dn, state_dn_conv, state_lru_h, state_lru_conv,
              w_in, dn_conv_w, dn_a_log, dn_dt_bias, dn_norm_g, lru_conv_w, lru_conv_b,
              lru_w_r, lru_b_r, lru_w_i, lru_b_i, lru_lambda, w_branch_a, w_branch_b, w_out,
              ln1_g, ln2_g, peer_w_q, peer_keys, peer_u, peer_v, final_g):
    bp = x_prompt.shape[0]
    dt = x_prompt.dtype
    xp, xs = x_prompt, x_sample
    dn_p, dnc_p, lh_p, lc_p = [], [], [], []
    dn_s, dnc_s, lh_s, lc_s = [], [], [], []
    for l in range(DEPTH):
        params = (w_in[l], dn_conv_w[l], dn_a_log[l], dn_dt_bias[l], dn_norm_g[l],
                  lru_conv_w[l], lru_conv_b[l], lru_w_r[l], lru_b_r[l], lru_w_i[l], lru_b_i[l],
                  lru_lambda[l], w_branch_a[l], w_branch_b[l], w_out[l], ln1_g[l], ln2_g[l],
                  peer_w_q[l], peer_keys[l], peer_u[l], peer_v[l])
        xp, S, db, hT, lb = layer(
            xp,
            jnp.zeros((bp, DN_HEADS, DN_DK, DN_DV), dt),
            jnp.zeros((bp, CONV_W - 1, DN_CONV_CH), dt),
            jnp.zeros((bp, LRU_WIDTH), dt),
            jnp.zeros((bp, CONV_W - 1, LRU_WIDTH), dt),
            *params)
        dn_p.append(S); dnc_p.append(db); lh_p.append(hT); lc_p.append(lb)
        xs, S, db, hT, lb = layer(xs, state_dn[l], state_dn_conv[l], state_lru_h[l], state_lru_conv[l], *params)
        dn_s.append(S); dnc_s.append(db); lh_s.append(hT); lc_s.append(lb)
    y_prompt = rmsnorm(xp, final_g)
    y_sample = rmsnorm(xs, final_g)
    new_dn_p = jnp.stack(dn_p)
    new_dn_conv_p = jnp.stack(dnc_p)
    new_lru_h_p = jnp.stack(lh_p)
    new_lru_conv_p = jnp.stack(lc_p)
    new_dn_s = jnp.stack(dn_s)
    new_dn_conv_s = jnp.stack(dnc_s)
    new_lru_h_s = jnp.stack(lh_s)
    new_lru_conv_s = jnp.stack(lc_s)
    return (y_prompt, y_sample, new_dn_p, new_dn_conv_p, new_lru_h_p, new_lru_conv_p,
            new_dn_s, new_dn_conv_s, new_lru_h_s, new_lru_conv_s)
```

```python
import functools
import math

import jax
import jax.numpy as jnp
import numpy as np
from jax import lax
from jax.experimental import pallas as pl
from jax.experimental.pallas import tpu as pltpu

F32 = jnp.float32
BF16 = jnp.bfloat16

D_MODEL = 1024
CONV_W = 4
EPS = 1e-6
DN_HEADS = 8
DN_DK = 128
DN_DV = 128
DN_QK = DN_HEADS * DN_DK
DN_V = DN_HEADS * DN_DV
DN_CONV_CH = 2 * DN_QK + DN_V
LRU_WIDTH = D_MODEL
LRU_BLOCKS = 8
LRU_BDIM = LRU_WIDTH // LRU_BLOCKS
LRU_C = 8.0
PEER_HEADS = 8
PEER_NKEYS = 128
PEER_DK = 128
PEER_DKH = PEER_DK // 2
PEER_TOPK = 16
CHUNK = 64

LANES = 128
SUBLANES = 8
MAIN_COLS = 8192
PROJ_COLS = MAIN_COLS + LANES
PROJ_COL_TILE = PROJ_COLS // 5
BD_BLOCK = MAIN_COLS // LANES
TAIL_ROW = SUBLANES - (CONV_W - 1)
VMEM_LIMIT = 56 * 1024 * 1024
NEG_INF = float("-inf")


def _params(*sem):
    return pltpu.CompilerParams(dimension_semantics=sem, vmem_limit_bytes=VMEM_LIMIT)


def _split3(x):
    a = x.astype(BF16)
    r = x - a.astype(F32)
    b = r.astype(BF16)
    c = (r - b.astype(F32)).astype(BF16)
    return a, b, c


def _dot(a, b, dims=None):
    if dims is None:
        return jnp.dot(a, b, preferred_element_type=F32)
    return lax.dot_general(a, b, (dims, ((), ())), preferred_element_type=F32)


def _dot3(a, b, dims=None):
    ah = a.astype(BF16)
    al = (a - ah.astype(F32)).astype(BF16)
    bh = b.astype(BF16)
    bl = (b - bh.astype(F32)).astype(BF16)
    return _dot(ah, bh, dims) + (_dot(ah, bl, dims) + _dot(al, bh, dims))


def _dot_exact_lhs(a_bf16, b, dims=None):
    b0, b1, b2 = _split3(b)
    return _dot(a_bf16, b0, dims) + (_dot(a_bf16, b1, dims) + _dot(a_bf16, b2, dims))


def _sigmoid(x):
    return 1.0 / (1.0 + jnp.exp(-x))


def _softplus(x):
    return jnp.maximum(x, 0.0) + jnp.log1p(jnp.exp(-jnp.abs(x)))


def _rms_scale(x):
    return lax.rsqrt(jnp.mean(x * x, axis=-1, keepdims=True) + EPS)


def _causal_conv(xpad_ref, new_rows, w, rows):
    xpad_ref[SUBLANES:SUBLANES + rows, :] = new_rows
    y = xpad_ref[TAIL_ROW:TAIL_ROW + rows, :] * w[0:1, :]
    for j in range(1, CONV_W):
        y = y + xpad_ref[TAIL_ROW + j:TAIL_ROW + j + rows, :] * w[j:j + 1, :]
    tail = xpad_ref[rows + TAIL_ROW:rows + SUBLANES, :]
    xpad_ref[TAIL_ROW:SUBLANES, :] = tail
    return y, tail


def _inproj_kernel(x_ref, g_ref, w_ref, o_ref):
    x = x_ref[...]
    n = (x * _rms_scale(x)) * g_ref[...]
    o_ref[...] = _dot(n.astype(BF16), w_ref[...])


def _in_proj(x, g, w_bf16, tm):
    n_tok = x.shape[0]
    return pl.pallas_call(
        _inproj_kernel,
        grid=(PROJ_COLS // PROJ_COL_TILE, n_tok // tm),
        in_specs=[
            pl.BlockSpec((tm, D_MODEL), lambda j, i: (i, 0)),
            pl.BlockSpec((1, D_MODEL), lambda j, i: (0, 0)),
            pl.BlockSpec((D_MODEL, PROJ_COL_TILE), lambda j, i: (0, j)),
        ],
        out_specs=pl.BlockSpec((tm, PROJ_COL_TILE), lambda j, i: (i, j)),
        out_shape=jax.ShapeDtypeStruct((n_tok, PROJ_COLS), F32),
        compiler_params=_params("arbitrary", "arbitrary"),
        name="in_proj",
    )(x, g, w_bf16)


def _unit_lower_inverse(nmat, rows, row_i, col_i):
    eye = jnp.where(row_i == col_i, 1.0, 0.0)
    blk = SUBLANES
    differ = row_i ^ col_i
    nd = jnp.where(differ < blk, nmat, 0.0)
    t = eye - nd
    m = _dot3(nd, nd)
    t = t + _dot3(t, m)
    m = _dot3(m, m)
    t = t + _dot3(t, m)
    while blk < rows:
        big = blk * 2
        off = jnp.where((differ >> int(math.log2(blk))) == 1, nmat, 0.0)
        t = t - _dot3(t, _dot3(off, t))
        blk = big
    return t


def _deltanet_kernel(qkv_ref, z_ref, bd_ref, s0_ref, buf0_ref, convw_ref, alog_ref, dtb_ref,
                     ng_ref, o_ref, s_ref, buf_ref, xpad_ref, *, rows):
    c = pl.program_id(1)

    @pl.when(c == 0)
    def _():
        xpad_ref[TAIL_ROW:SUBLANES, :] = buf0_ref[0]
        s_ref[...] = s0_ref[...]

    y, tail = _causal_conv(xpad_ref, qkv_ref[0], convw_ref[...], rows)
    buf_ref[0] = tail
    act = y * _sigmoid(y)

    bd = bd_ref[0]
    beta_all = _sigmoid(bd)
    g_all = -jnp.exp(alog_ref[...]) * _softplus(bd + dtb_ref[...])
    row_i = lax.broadcasted_iota(jnp.int32, (rows, rows), 0)
    col_i = lax.broadcasted_iota(jnp.int32, (rows, rows), 1)
    incl = row_i >= col_i
    strict = row_i > col_i
    tri = jnp.where(incl, 1.0, 0.0).astype(BF16)
    gcum = _dot_exact_lhs(tri, g_all)
    gcum_t = jnp.concatenate([gcum, jnp.zeros((LANES - rows, LANES), F32)], axis=0).T

    for h in range(DN_HEADS):
        lo = h * DN_DK
        q = act[:, lo:lo + DN_DK]
        k = act[:, DN_QK + lo:DN_QK + lo + DN_DK]
        v = act[:, 2 * DN_QK + lo:2 * DN_QK + lo + DN_DV]
        q = q * (lax.rsqrt(jnp.sum(q * q, axis=-1, keepdims=True) + EPS) * (DN_DK ** -0.5))
        k = k * lax.rsqrt(jnp.sum(k * k, axis=-1, keepdims=True) + EPS)
        beta = beta_all[:, h:h + 1]
        gc = gcum[:, DN_HEADS + h:DN_HEADS + h + 1]
        gr = gcum_t[DN_HEADS + h:DN_HEADS + h + 1, :rows]
        g_last = gc[rows - 1:rows, :]
        dec = jnp.where(incl, jnp.exp(jnp.where(incl, gc - gr, 0.0)), 0.0)
        kk = _dot3(k, k, ((1,), (1,)))
        nmat = jnp.where(strict, beta * kk * dec, 0.0)
        gam = jnp.exp(gc)
        rhs = jnp.concatenate([(beta * gam) * k, beta * v], axis=-1)
        sol = _dot3(_unit_lower_inverse(nmat, rows, row_i, col_i), rhs)
        wm = sol[:, :DN_DK]
        uv = sol[:, DN_DK:]
        qk = _dot3(q, k, ((1,), (1,))) * dec
        qg = q * gam
        kend = k * jnp.exp(g_last - gc)
        s_old = s_ref[0, h]
        ws = _dot3(jnp.concatenate([wm, qg], axis=0), s_old)
        u = uv - ws[:rows]
        o = ws[rows:] + _dot3(qk, u)
        s_ref[0, h] = jnp.exp(g_last) * s_old + _dot3(kend, u, ((0,), (0,)))
        zh = z_ref[0, :, lo:lo + DN_DV]
        o = (o * _rms_scale(o)) * ng_ref[...] * (zh * _sigmoid(zh))
        o_ref[0, :, lo:lo + DN_DV] = o


def _deltanet(proj3, s0, buf0, convw, alog_row, dtb_row, norm_g, rows):
    b, t, _ = proj3.shape
    nc = t // rows
    return pl.pallas_call(
        functools.partial(_deltanet_kernel, rows=rows),
        grid=(b, nc),
        in_specs=[
            pl.BlockSpec((1, rows, DN_CONV_CH), lambda i, c: (i, c, 0)),
            pl.BlockSpec((1, rows, DN_V), lambda i, c: (i, c, DN_CONV_CH // DN_V)),
            pl.BlockSpec((1, rows, LANES), lambda i, c: (i, c, BD_BLOCK)),
            pl.BlockSpec((1, DN_HEADS, DN_DK, DN_DV), lambda i, c: (i, 0, 0, 0)),
            pl.BlockSpec((1, CONV_W - 1, DN_CONV_CH), lambda i, c: (i, 0, 0)),
            pl.BlockSpec((CONV_W, DN_CONV_CH), lambda i, c: (0, 0)),
            pl.BlockSpec((1, LANES), lambda i, c: (0, 0)),
            pl.BlockSpec((1, LANES), lambda i, c: (0, 0)),
            pl.BlockSpec((1, DN_DV), lambda i, c: (0, 0)),
        ],
        out_specs=[
            pl.BlockSpec((1, rows, DN_V), lambda i, c: (i, c, 0)),
            pl.BlockSpec((1, DN_HEADS, DN_DK, DN_DV), lambda i, c: (i, 0, 0, 0)),
            pl.BlockSpec((1, CONV_W - 1, DN_CONV_CH), lambda i, c: (i, 0, 0)),
        ],
        out_shape=[
            jax.ShapeDtypeStruct((b, t, DN_V), F32),
            jax.ShapeDtypeStruct((b, DN_HEADS, DN_DK, DN_DV), F32),
            jax.ShapeDtypeStruct((b, CONV_W - 1, DN_CONV_CH), F32),
        ],
        scratch_shapes=[pltpu.VMEM((SUBLANES + rows, DN_CONV_CH), F32)],
        compiler_params=_params("arbitrary", "arbitrary"),
        name="deltanet",
    )(proj3, proj3, proj3, s0, buf0, convw, alog_row, dtb_row, norm_g)


def _rglru_kernel(lx_ref, ly_ref, h0_ref, buf0_ref, convw_ref, convb_ref, wr_ref, br_ref,
                  wi_ref, bi_ref, lam_ref, o_ref, h_ref, buf_ref, xpad_ref, a_ref, b_ref, *, rows):
    c = pl.program_id(1)

    @pl.when(c == 0)
    def _():
        xpad_ref[TAIL_ROW:SUBLANES, :] = buf0_ref[0]
        h_ref[...] = h0_ref[...]

    y, tail = _causal_conv(xpad_ref, lx_ref[0], convw_ref[...], rows)
    buf_ref[0] = tail
    x = y + convb_ref[...]
    xb = x.astype(BF16)
    sp = LRU_C * _softplus(-lam_ref[...])
    for n in range(LRU_BLOCKS):
        lo = n * LRU_BDIM
        xn = xb[:, lo:lo + LRU_BDIM]
        r = _sigmoid(_dot(xn, wr_ref[n]) + br_ref[:, lo:lo + LRU_BDIM])
        i = _sigmoid(_dot(xn, wi_ref[n]) + bi_ref[:, lo:lo + LRU_BDIM])
        log_a = -(r * sp[:, lo:lo + LRU_BDIM])
        a = jnp.exp(log_a)
        a_ref[:, lo:lo + LRU_BDIM] = a
        one_minus_a2 = -jnp.tanh(log_a) * (a * a + 1.0)
        b_ref[:, lo:lo + LRU_BDIM] = jnp.sqrt(one_minus_a2) * (i * x[:, lo:lo + LRU_BDIM])

    groups = rows // SUBLANES
    a = a_ref[...].reshape(groups, SUBLANES, LRU_WIDTH)
    b = b_ref[...].reshape(groups, SUBLANES, LRU_WIDTH)
    sub = lax.broadcasted_iota(jnp.int32, (groups, SUBLANES, LRU_WIDTH), 1)
    d = 1
    while d < SUBLANES:
        keep = sub >= d
        a_prev = jnp.where(keep, pltpu.roll(a, d, 1), 1.0)
        b_prev = jnp.where(keep, pltpu.roll(b, d, 1), 0.0)
        b = a * b_prev + b
        a = a * a_prev
        d *= 2
    a_ref[...] = a.reshape(rows, LRU_WIDTH)
    b_ref[...] = b.reshape(rows, LRU_WIDTH)

    def group_step(g, h):
        r0 = pl.multiple_of(g * SUBLANES, SUBLANES)
        hs = a_ref[pl.ds(r0, SUBLANES), :] * h + b_ref[pl.ds(r0, SUBLANES), :]
        yl = ly_ref[0, pl.ds(r0, SUBLANES), :]
        o_ref[0, pl.ds(r0, SUBLANES), :] = hs * jax.nn.gelu(yl, approximate=True)
        return hs[SUBLANES - 1:SUBLANES, :]

    h_ref[0] = lax.fori_loop(0, groups, group_step, h_ref[0])


def _rglru(proj3, h0, buf0, convw, convb, w_r, b_r, w_i, b_i, lam, rows):
    b, t, _ = proj3.shape
    nc = t // rows
    row = lambda i, c: (0, 0)
    return pl.pallas_call(
        functools.partial(_rglru_kernel, rows=rows),
        grid=(b, nc),
        in_specs=[
            pl.BlockSpec((1, rows, LRU_WIDTH), lambda i, c: (i, c, 4)),
            pl.BlockSpec((1, rows, LRU_WIDTH), lambda i, c: (i, c, 5)),
            pl.BlockSpec((1, 1, LRU_WIDTH), lambda i, c: (i, 0, 0)),
            pl.BlockSpec((1, CONV_W - 1, LRU_WIDTH), lambda i, c: (i, 0, 0)),
            pl.BlockSpec((CONV_W, LRU_WIDTH), row),
            pl.BlockSpec((1, LRU_WIDTH), row),
            pl.BlockSpec((LRU_BLOCKS, LRU_BDIM, LRU_BDIM), lambda i, c: (0, 0, 0)),
            pl.BlockSpec((1, LRU_WIDTH), row),
            pl.BlockSpec((LRU_BLOCKS, LRU_BDIM, LRU_BDIM), lambda i, c: (0, 0, 0)),
            pl.BlockSpec((1, LRU_WIDTH), row),
            pl.BlockSpec((1, LRU_WIDTH), row),
        ],
        out_specs=[
            pl.BlockSpec((1, rows, LRU_WIDTH), lambda i, c: (i, c, 0)),
            pl.BlockSpec((1, 1, LRU_WIDTH), lambda i, c: (i, 0, 0)),
            pl.BlockSpec((1, CONV_W - 1, LRU_WIDTH), lambda i, c: (i, 0, 0)),
        ],
        out_shape=[
            jax.ShapeDtypeStruct((b, t, LRU_WIDTH), F32),
            jax.ShapeDtypeStruct((b, 1, LRU_WIDTH), F32),
            jax.ShapeDtypeStruct((b, CONV_W - 1, LRU_WIDTH), F32),
        ],
        scratch_shapes=[
            pltpu.VMEM((SUBLANES + rows, LRU_WIDTH), F32),
            pltpu.VMEM((rows, LRU_WIDTH), F32),
            pltpu.VMEM((rows, LRU_WIDTH), F32),
        ],
        compiler_params=_params("arbitrary", "arbitrary"),
        name="rglru",
    )(proj3, proj3, h0, buf0, convw, convb, w_r, b_r, w_i, b_i, lam)


def _merge_kernel(x_ref, oa_ref, ob_ref, ga_ref, gb_ref, wa_ref, wb_ref, wo_ref, x1_ref):
    ya = _dot(oa_ref[...].astype(BF16), wa_ref[...])
    yb = _dot(ob_ref[...].astype(BF16), wb_ref[...])
    mix = _sigmoid(ga_ref[...]) * ya + _sigmoid(gb_ref[...]) * yb
    x1_ref[...] = x_ref[...] + _dot(mix.astype(BF16), wo_ref[...])


def _merge(x, o_a, o_b, proj, w_a, w_b, w_o, tm):
    n_tok = x.shape[0]
    tok = lambda i: (i, 0)
    full = lambda i: (0, 0)
    return pl.pallas_call(
        _merge_kernel,
        grid=(n_tok // tm,),
        in_specs=[
            pl.BlockSpec((tm, D_MODEL), tok),
            pl.BlockSpec((tm, DN_V), tok),
            pl.BlockSpec((tm, LRU_WIDTH), tok),
            pl.BlockSpec((tm, D_MODEL), lambda i: (i, 6)),
            pl.BlockSpec((tm, D_MODEL), lambda i: (i, 7)),
            pl.BlockSpec((DN_V, D_MODEL), full),
            pl.BlockSpec((LRU_WIDTH, D_MODEL), full),
            pl.BlockSpec((D_MODEL, D_MODEL), full),
        ],
        out_specs=pl.BlockSpec((tm, D_MODEL), tok),
        out_shape=jax.ShapeDtypeStruct((n_tok, D_MODEL), F32),
        compiler_params=_params("arbitrary"),
        name="merge",
    )(x, o_a, o_b, proj, proj, w_a, w_b, w_o)


def _candidate_slabs():
    return (
        ("row", 0, 0, 16, 0), ("row", 1, 0, 8, 0), ("row", 2, 0, 8, 0), ("row", 3, 0, 8, 0),
        ("col", 0, 8, 8, 0), ("col", 0, 0, 8, 4), ("col", 1, 0, 8, 4), ("col", 2, 0, 8, 4),
    )


def _candidate_positions():
    pos = []
    for kind, fixed, start, count, first_valid in _candidate_slabs():
        for r in range(start, start + count):
            i, j = (fixed, r) if kind == "row" else (r, fixed)
            masked = r - start < first_valid
            pos.append(PEER_TOPK * PEER_TOPK + len(pos) if masked else i * PEER_TOPK + j)
    assert len(set(pos)) == len(pos)
    return np.asarray(pos, np.int32).reshape(-1, 1)


def _top16(s, vals_ref, key_iota):
    big = jnp.int32(1 << 20)

    def body(r, carry):
        s, rank = carry
        m = jnp.max(s, axis=0, keepdims=True)
        first = jnp.min(jnp.where(s == m, key_iota, big), axis=0, keepdims=True)
        sel = key_iota == first
        vals_ref[pl.ds(r, 1), :] = m
        return jnp.where(sel, NEG_INF, s), jnp.where(sel, r, rank)

    rank0 = jnp.full(s.shape, PEER_TOPK, jnp.int32)
    _, rank = lax.fori_loop(0, PEER_TOPK, body, (s, rank0))
    return rank


def _retrieve_kernel(x1_ref, g_ref, wqh_ref, wql_ref, kblk_ref, pos_ref,
                     n2_ref, lena_ref, ca_ref, rankb_ref, pb_ref, qt_ref, v1_ref, v2_ref, *, tm):
    x1 = x1_ref[...]
    n2 = (x1 * _rms_scale(x1)) * g_ref[...]
    nh = n2.astype(BF16)
    nl = (n2 - nh.astype(F32)).astype(BF16)
    n2_ref[...] = nh
    nt = ((1,), (1,))
    qt_ref[...] = _dot(wqh_ref[...], nh, nt) + (_dot(wqh_ref[...], nl, nt) + _dot(wql_ref[...], nh, nt))

    key_iota = lax.broadcasted_iota(jnp.int32, (PEER_NKEYS, tm), 0)
    rank_iota = lax.broadcasted_iota(jnp.int32, (PEER_TOPK, tm), 0)
    pos = pos_ref[...]
    big = jnp.int32(1 << 20)

    def head(h, _):
        q = qt_ref[pl.ds(pl.multiple_of(h * PEER_DK, PEER_DK), PEER_DK), :]
        st = _dot3(kblk_ref[h], q)
        s1 = st[:PEER_NKEYS]
        s2 = st[PEER_NKEYS:]
        rank1 = _top16(s1, v1_ref, key_iota)
        rank2 = _top16(s2, v2_ref, key_iota)
        v1 = v1_ref[...]
        v2 = v2_ref[...]
        pieces = []
        for kind, fixed, start, count, first_valid in _candidate_slabs():
            if kind == "row":
                piece = v1[fixed:fixed + 1, :] + v2[start:start + count, :]
            else:
                piece = v1[start:start + count, :] + v2[fixed:fixed + 1, :]
            if first_valid:
                local = lax.broadcasted_iota(jnp.int32, piece.shape, 0)
                piece = jnp.where(local >= first_valid, piece, NEG_INF)
            pieces.append(piece)
        cand = jnp.concatenate(pieces, axis=0)
        t0 = v1[0:1, :] + v2[0:1, :]

        def pick(_, carry):
            cand, len_i, z = carry
            m = jnp.max(cand, axis=0, keepdims=True)
            first = jnp.min(jnp.where(cand == m, pos, big), axis=0, keepdims=True)
            cand = jnp.where(pos == first, NEG_INF, cand)
            len_i = len_i + (rank_iota == (first >> 4)).astype(F32)
            return cand, len_i, z + jnp.exp(m - t0)

        _, len_i, z = lax.fori_loop(
            0, PEER_TOPK, pick,
            (cand, jnp.zeros((PEER_TOPK, tm), F32), jnp.zeros((1, tm), F32)))

        len_a = jnp.zeros((PEER_NKEYS, tm), F32)
        for i in range(PEER_TOPK):
            len_a = jnp.where(rank1 == i, len_i[i:i + 1, :], len_a)
        lena_ref[h] = len_a
        ca_ref[h] = jnp.exp(s1 - v1[0:1, :]) / z
        rankb_ref[h] = rank2.astype(F32)
        pb_ref[h] = jnp.exp(s2 - v2[0:1, :])
        return 0

    lax.fori_loop(0, PEER_HEADS, head, 0)


def _retrieve(x1, g2, wq_hi, wq_lo, kblk, tm):
    n_tok = x1.shape[0]
    pos = jnp.asarray(_candidate_positions())
    per_key = jax.ShapeDtypeStruct((PEER_HEADS, PEER_NKEYS, n_tok), F32)
    key_spec = pl.BlockSpec((PEER_HEADS, PEER_NKEYS, tm), lambda i: (0, 0, i))
    return pl.pallas_call(
        functools.partial(_retrieve_kernel, tm=tm),
        grid=(n_tok // tm,),
        in_specs=[
            pl.BlockSpec((tm, D_MODEL), lambda i: (i, 0)),
            pl.BlockSpec((1, D_MODEL), lambda i: (0, 0)),
            pl.BlockSpec((D_MODEL, D_MODEL), lambda i: (0, 0)),
            pl.BlockSpec((D_MODEL, D_MODEL), lambda i: (0, 0)),
            pl.BlockSpec((PEER_HEADS, 2 * PEER_NKEYS, PEER_DK), lambda i: (0, 0, 0)),
            pl.BlockSpec(pos.shape, lambda i: (0, 0)),
        ],
        out_specs=[pl.BlockSpec((tm, D_MODEL), lambda i: (i, 0)), key_spec, key_spec, key_spec, key_spec],
        out_shape=[jax.ShapeDtypeStruct((n_tok, D_MODEL), BF16), per_key, per_key, per_key, per_key],
        scratch_shapes=[
            pltpu.VMEM((D_MODEL, tm), F32),
            pltpu.VMEM((PEER_TOPK, tm), F32),
            pltpu.VMEM((PEER_TOPK, tm), F32),
        ],
        compiler_params=_params("arbitrary"),
        name="retrieve",
    )(x1, g2, wq_hi, wq_lo, kblk, pos)


EXPERT_TILE = 1024
A_PER_TILE = EXPERT_TILE // PEER_NKEYS


def _experts_kernel(x1_ref, n2_ref, u_ref, vt_ref, lena_ref, ca_ref, rankb_ref, pb_ref, fg_ref,
                    y_ref, acc_ref, s_ref, act_ref, *, tm):
    j = pl.program_id(1)

    @pl.when(j == 0)
    def _():
        acc_ref[...] = jnp.zeros_like(acc_ref)

    s_ref[...] = _dot(u_ref[...], n2_ref[...], ((1,), (1,)))

    def per_a(al, _):
        r0 = pl.multiple_of(al * PEER_NKEYS, PEER_NKEYS)
        s = s_ref[pl.ds(r0, PEER_NKEYS), :]
        w = jnp.zeros((PEER_NKEYS, tm), F32)
        for h in range(PEER_HEADS):
            len_a = lena_ref[h, pl.ds(al, 1), :]
            c_a = ca_ref[h, pl.ds(al, 1), :]
            w = w + jnp.where(rankb_ref[h] < len_a, pb_ref[h], 0.0) * c_a
        act = 0.5 * s * (1.0 + lax.erf(s * (1.0 / math.sqrt(2.0))))
        act_ref[pl.ds(r0, PEER_NKEYS), :] = (act * w).astype(BF16)
        return 0

    lax.fori_loop(0, A_PER_TILE, per_a, 0)
    acc_ref[...] += _dot(vt_ref[...], act_ref[...])

    @pl.when(j == pl.num_programs(1) - 1)
    def _():
        x2 = x1_ref[...] + acc_ref[...].T
        y_ref[...] = (x2 * _rms_scale(x2)) * fg_ref[...]


def _experts(x1, n2, u_bf16, vt_bf16, len_a, c_a, rank_b, p_b, final_g, tm):
    n_tok = x1.shape[0]
    n_exp = u_bf16.shape[0]
    a_spec = pl.BlockSpec((PEER_HEADS, A_PER_TILE, tm), lambda i, j: (0, j, i))
    b_spec = pl.BlockSpec((PEER_HEADS, PEER_NKEYS, tm), lambda i, j: (0, 0, i))
    return pl.pallas_call(
        functools.partial(_experts_kernel, tm=tm),
        grid=(n_tok // tm, n_exp // EXPERT_TILE),
        in_specs=[
            pl.BlockSpec((tm, D_MODEL), lambda i, j: (i, 0)),
            pl.BlockSpec((tm, D_MODEL), lambda i, j: (i, 0)),
            pl.BlockSpec((EXPERT_TILE, D_MODEL), lambda i, j: (j, 0)),
            pl.BlockSpec((D_MODEL, EXPERT_TILE), lambda i, j: (0, j)),
            a_spec, a_spec, b_spec, b_spec,
            pl.BlockSpec((1, D_MODEL), lambda i, j: (0, 0)),
        ],
        out_specs=pl.BlockSpec((tm, D_MODEL), lambda i, j: (i, 0)),
        out_shape=jax.ShapeDtypeStruct((n_tok, D_MODEL), F32),
        scratch_shapes=[
            pltpu.VMEM((D_MODEL, tm), F32),
            pltpu.VMEM((EXPERT_TILE, tm), F32),
            pltpu.VMEM((EXPERT_TILE, tm), BF16),
        ],
        compiler_params=_params("arbitrary", "arbitrary"),
        name="experts",
    )(x1, n2, u_bf16, vt_bf16, len_a, c_a, rank_b, p_b, final_g)


def _token_tile(n_tok, cap):
    tm = cap
    while n_tok % tm:
        tm //= 2
    return tm


def _stream(x, s0, dn_buf0, h0, lru_buf0, p):
    b, t, _ = x.shape
    n_tok = b * t
    rows = min(CHUNK, t)
    xf = x.reshape(n_tok, D_MODEL)
    proj = _in_proj(xf, p["ln1_g"], p["w_in"], _token_tile(n_tok, 1024))
    proj3 = proj.reshape(b, t, PROJ_COLS)
    o_a, s_new, dn_buf = _deltanet(proj3, s0, dn_buf0, p["dn_conv_w"], p["alog_row"], p["dtb_row"],
                                   p["dn_norm_g"], rows)
    lru_rows = _token_tile(t, 256)
    o_b, h_new, lru_buf = _rglru(proj3, h0.reshape(b, 1, LRU_WIDTH), lru_buf0, p["lru_conv_w"],
                                 p["lru_conv_b"], p["lru_w_r"], p["lru_b_r"], p["lru_w_i"],
                                 p["lru_b_i"], p["lru_lambda"], lru_rows)
    x1 = _merge(xf, o_a.reshape(n_tok, DN_V), o_b.reshape(n_tok, LRU_WIDTH), proj,
                p["w_branch_a"], p["w_branch_b"], p["w_out"], _token_tile(n_tok, 512))
    tr = _token_tile(n_tok, 256)
    n2, len_a, c_a, rank_b, p_b = _retrieve(x1, p["ln2_g"], p["wq_hi"], p["wq_lo"], p["kblk"], tr)
    y = _experts(x1, n2, p["peer_u"], p["peer_vt"], len_a, c_a, rank_b, p_b, p["final_g"],
                 _token_tile(n_tok, 512))
    return (y.reshape(b, t, D_MODEL), s_new, dn_buf, h_new.reshape(b, LRU_WIDTH), lru_buf)


def _prepare(w_in, dn_conv_w, dn_a_log, dn_dt_bias, dn_norm_g, lru_conv_w, lru_conv_b, lru_w_r,
             lru_b_r, lru_w_i, lru_b_i, lru_lambda, w_branch_a, w_branch_b, w_out, ln1_g, ln2_g,
             peer_w_q, peer_keys, peer_u, peer_v, final_g):
    row = lambda v: v.reshape(1, -1).astype(F32)
    lane_row = lambda v: jnp.zeros((1, LANES), F32).at[0, DN_HEADS:2 * DN_HEADS].set(v)
    w_in_p = jnp.pad(w_in, ((0, 0), (0, PROJ_COLS - w_in.shape[1]))).astype(BF16)
    wq_t = peer_w_q.T
    wq_hi = wq_t.astype(BF16)
    wq_lo = (wq_t - wq_hi.astype(F32)).astype(BF16)
    zeros = jnp.zeros((PEER_HEADS, PEER_NKEYS, PEER_DKH), F32)
    kblk = jnp.concatenate([
        jnp.concatenate([peer_keys[:, 0], zeros], axis=-1),
        jnp.concatenate([zeros, peer_keys[:, 1]], axis=-1)], axis=1)
    return dict(
        w_in=w_in_p, dn_conv_w=dn_conv_w, alog_row=lane_row(dn_a_log), dtb_row=lane_row(dn_dt_bias),
        dn_norm_g=row(dn_norm_g), lru_conv_w=lru_conv_w, lru_conv_b=row(lru_conv_b),
        lru_w_r=lru_w_r.astype(BF16), lru_b_r=row(lru_b_r), lru_w_i=lru_w_i.astype(BF16),
        lru_b_i=row(lru_b_i), lru_lambda=row(lru_lambda), w_branch_a=w_branch_a.astype(BF16),
        w_branch_b=w_branch_b.astype(BF16), w_out=w_out.astype(BF16), ln1_g=row(ln1_g),
        ln2_g=row(ln2_g), wq_hi=wq_hi, wq_lo=wq_lo, kblk=kblk, peer_u=peer_u.astype(BF16),
        peer_vt=peer_v.T.astype(BF16), final_g=row(final_g))


def kernel(x_prompt, x_sample, state_dn, state_dn_conv, state_lru_h, state_lru_conv, w_in, dn_conv_w, dn_a_log, dn_dt_bias, dn_norm_g, lru_conv_w, lru_conv_b, lru_w_r, lru_b_r, lru_w_i, lru_b_i, lru_lambda, w_branch_a, w_branch_b, w_out, ln1_g, ln2_g, peer_w_q, peer_keys, peer_u, peer_v, final_g):
    depth = w_in.shape[0]
    assert depth == 1, "the final norm is fused into the layer's last kernel"
    bp = x_prompt.shape[0]
    dt = x_prompt.dtype
    p = _prepare(w_in[0], dn_conv_w[0], dn_a_log[0], dn_dt_bias[0], dn_norm_g[0], lru_conv_w[0],
                 lru_conv_b[0], lru_w_r[0], lru_b_r[0], lru_w_i[0], lru_b_i[0], lru_lambda[0],
                 w_branch_a[0], w_branch_b[0], w_out[0], ln1_g[0], ln2_g[0], peer_w_q[0],
                 peer_keys[0], peer_u[0], peer_v[0], final_g)
    prompt = _stream(
        x_prompt,
        jnp.zeros((bp, DN_HEADS, DN_DK, DN_DV), dt),
        jnp.zeros((bp, CONV_W - 1, DN_CONV_CH), dt),
        jnp.zeros((bp, LRU_WIDTH), dt),
        jnp.zeros((bp, CONV_W - 1, LRU_WIDTH), dt),
        p)
    sample = _stream(x_sample, state_dn[0], state_dn_conv[0], state_lru_h[0], state_lru_conv[0], p)
    y_p, s_p, db_p, h_p, lb_p = prompt
    y_s, s_s, db_s, h_s, lb_s = sample
    return (y_p, y_s, s_p[None], db_p[None], h_p[None], lb_p[None],
            s_s[None], db_s[None], h_s[None], lb_s[None])
```

```python
import functools
import math

import jax
import jax.numpy as jnp
import numpy as np
from jax import lax
from jax.experimental import pallas as pl
from jax.experimental.pallas import tpu as pltpu

F32 = jnp.float32
BF16 = jnp.bfloat16

D_MODEL = 1024
CONV_W = 4
EPS = 1e-6
DN_HEADS = 8
DN_DK = 128
DN_DV = 128
DN_QK = DN_HEADS * DN_DK
DN_V = DN_HEADS * DN_DV
DN_CONV_CH = 2 * DN_QK + DN_V
LRU_WIDTH = D_MODEL
LRU_BLOCKS = 8
LRU_BDIM = LRU_WIDTH // LRU_BLOCKS
LRU_C = 8.0
PEER_HEADS = 8
PEER_NKEYS = 128
PEER_DK = 128
PEER_DKH = PEER_DK // 2
PEER_TOPK = 16
CHUNK = 64

LANES = 128
SUBLANES = 8
MAIN_COLS = 8192
PROJ_COLS = MAIN_COLS + LANES
PROJ_COL_TILE = PROJ_COLS // 5
BD_BLOCK = MAIN_COLS // LANES
TAIL_ROW = SUBLANES - (CONV_W - 1)
VMEM_LIMIT = 56 * 1024 * 1024
NEG_INF = float("-inf")


def _params(*sem):
    return pltpu.CompilerParams(dimension_semantics=sem, vmem_limit_bytes=VMEM_LIMIT)


def _split3(x):
    a = x.astype(BF16)
    r = x - a.astype(F32)
    b = r.astype(BF16)
    c = (r - b.astype(F32)).astype(BF16)
    return a, b, c


def _dot(a, b, dims=None):
    if dims is None:
        return jnp.dot(a, b, preferred_element_type=F32)
    return lax.dot_general(a, b, (dims, ((), ())), preferred_element_type=F32)


def _dot3(a, b, dims=None):
    ah = a.astype(BF16)
    al = (a - ah.astype(F32)).astype(BF16)
    bh = b.astype(BF16)
    bl = (b - bh.astype(F32)).astype(BF16)
    return _dot(ah, bh, dims) + (_dot(ah, bl, dims) + _dot(al, bh, dims))


def _dot_exact_lhs(a_bf16, b, dims=None):
    b0, b1, b2 = _split3(b)
    return _dot(a_bf16, b0, dims) + (_dot(a_bf16, b1, dims) + _dot(a_bf16, b2, dims))


def _sigmoid(x):
    return 1.0 / (1.0 + jnp.exp(-x))


def _softplus(x):
    return jnp.maximum(x, 0.0) + jnp.log1p(jnp.exp(-jnp.abs(x)))


def _rms_scale(x):
    return lax.rsqrt(jnp.mean(x * x, axis=-1, keepdims=True) + EPS)


def _causal_conv(xpad_ref, new_rows, w, rows):
    xpad_ref[SUBLANES:SUBLANES + rows, :] = new_rows
    y = xpad_ref[TAIL_ROW:TAIL_ROW + rows, :] * w[0:1, :]
    for j in range(1, CONV_W):
        y = y + xpad_ref[TAIL_ROW + j:TAIL_ROW + j + rows, :] * w[j:j + 1, :]
    tail = xpad_ref[rows + TAIL_ROW:rows + SUBLANES, :]
    xpad_ref[TAIL_ROW:SUBLANES, :] = tail
    return y, tail


def _inproj_kernel(x_ref, g_ref, w_ref, o_ref):
    x = x_ref[...]
    n = (x * _rms_scale(x)) * g_ref[...]
    o_ref[...] = _dot(n.astype(BF16), w_ref[...])


def _in_proj(x, g, w_bf16, tm):
    n_tok = x.shape[0]
    return pl.pallas_call(
        _inproj_kernel,
        grid=(PROJ_COLS // PROJ_COL_TILE, n_tok // tm),
        in_specs=[
            pl.BlockSpec((tm, D_MODEL), lambda j, i: (i, 0)),
            pl.BlockSpec((1, D_MODEL), lambda j, i: (0, 0)),
            pl.BlockSpec((D_MODEL, PROJ_COL_TILE), lambda j, i: (0, j)),
        ],
        out_specs=pl.BlockSpec((tm, PROJ_COL_TILE), lambda j, i: (i, j)),
        out_shape=jax.ShapeDtypeStruct((n_tok, PROJ_COLS), F32),
        compiler_params=_params("arbitrary", "arbitrary"),
        name="in_proj",
    )(x, g, w_bf16)


def _bdot(a, b, eq):
    return jnp.einsum(eq, a, b, preferred_element_type=F32)


def _bdot1(a, b, eq):
    return _bdot(a.astype(BF16), b.astype(BF16), eq)


def _bdot3(a, b, eq):
    ah = a.astype(BF16)
    al = (a - ah.astype(F32)).astype(BF16)
    bh = b.astype(BF16)
    bl = (b - bh.astype(F32)).astype(BF16)
    return _bdot(ah, bh, eq) + (_bdot(ah, bl, eq) + _bdot(al, bh, eq))


MM = "gij,gjk->gik"
DN_CHUNKS_PER_STEP = 4


def _unit_lower_inverse(nmat, rows, row_i, col_i):
    eye = jnp.where(row_i == col_i, 1.0, 0.0)
    blk = SUBLANES
    differ = row_i ^ col_i
    nd = jnp.where(differ < blk, nmat, 0.0)
    t = eye - nd
    m = _bdot3(nd, nd, MM)
    t = t + _bdot3(t, m, MM)
    m = _bdot3(m, m, MM)
    t = t + _bdot3(t, m, MM)
    while blk < rows:
        off = jnp.where((differ >> int(math.log2(blk))) == 1, nmat, 0.0)
        t = t - _bdot3(t, _bdot3(off, t, MM), MM)
        blk *= 2
    return t


def _deltanet_kernel(qkv_ref, z_ref, bd_ref, s0_ref, buf0_ref, convw_ref, alog_ref, dtb_ref,
                     ng_ref, o_ref, s_ref, buf_ref, xpad_ref, *, rows, chunks):
    span = rows * chunks

    @pl.when(pl.program_id(1) == 0)
    def _():
        xpad_ref[TAIL_ROW:SUBLANES, :] = buf0_ref[0]
        s_ref[...] = s0_ref[...]

    y, tail = _causal_conv(xpad_ref, qkv_ref[0], convw_ref[...], span)
    buf_ref[0] = tail
    act = y * _sigmoid(y)

    bd = bd_ref[0]
    beta_all = _sigmoid(bd)
    g_all = -jnp.exp(alog_ref[...]) * _softplus(bd + dtb_ref[...])
    t_row = lax.broadcasted_iota(jnp.int32, (span, span), 0)
    t_col = lax.broadcasted_iota(jnp.int32, (span, span), 1)
    tri = jnp.where((t_row ^ t_col) < rows, jnp.where(t_row >= t_col, 1.0, 0.0), 0.0).astype(BF16)
    gcum = _dot_exact_lhs(tri, g_all)
    pad = -span % LANES
    gcum_t = (jnp.concatenate([gcum, jnp.zeros((pad, LANES), F32)], axis=0) if pad else gcum).T

    def per_head(src, col0, width):
        return jnp.stack([src[c * rows:(c + 1) * rows, col0 + h * width:col0 + (h + 1) * width]
                          for c in range(chunks) for h in range(DN_HEADS)])

    q = per_head(act, 0, DN_DK)
    k = per_head(act, DN_QK, DN_DK)
    v = per_head(act, 2 * DN_QK, DN_DV)
    z = per_head(z_ref[0], 0, DN_DV)
    beta = per_head(beta_all, 0, 1)
    gc = per_head(gcum, DN_HEADS, 1)
    gr = jnp.stack([gcum_t[DN_HEADS + h:DN_HEADS + h + 1, c * rows:(c + 1) * rows]
                    for c in range(chunks) for h in range(DN_HEADS)])
    g_last = gc[:, rows - 1:rows, :]

    q = q * (lax.rsqrt(jnp.sum(q * q, axis=-1, keepdims=True) + EPS) * (DN_DK ** -0.5))
    k = k * lax.rsqrt(jnp.sum(k * k, axis=-1, keepdims=True) + EPS)
    row_i = lax.broadcasted_iota(jnp.int32, (rows, rows), 0)
    col_i = lax.broadcasted_iota(jnp.int32, (rows, rows), 1)
    incl = row_i >= col_i
    dec = jnp.where(incl, jnp.exp(jnp.where(incl, gc - gr, 0.0)), 0.0)
    kb = k.astype(BF16)
    kk = _bdot(kb, kb, "gik,gjk->gij")
    nmat = jnp.where(row_i > col_i, beta * kk * dec, 0.0)
    gam = jnp.exp(gc)
    rhs = jnp.concatenate([(beta * gam) * k, beta * v], axis=-1)
    sol = _bdot1(_unit_lower_inverse(nmat, rows, row_i, col_i), rhs, MM)
    uv = sol[:, :, DN_DK:]
    wq = jnp.concatenate([sol[:, :, :DN_DK], q * gam], axis=1)
    qk = _bdot(q.astype(BF16), kb, "gik,gjk->gij") * dec
    kend = k * jnp.exp(g_last - gc)
    gend = jnp.exp(g_last)
    gate = ng_ref[...] * (z * _sigmoid(z))

    for c in range(chunks):
        hs = slice(c * DN_HEADS, (c + 1) * DN_HEADS)
        s_old = s_ref[0]
        ws = _bdot1(wq[hs], s_old, MM)
        u = uv[hs] - ws[:, :rows]
        ub = u.astype(BF16)
        o = ws[:, rows:] + _bdot(qk[hs].astype(BF16), ub, MM)
        s_ref[0] = gend[hs] * s_old + _bdot(kend[hs].astype(BF16), ub, "glk,glv->gkv")
        o = (o * _rms_scale(o)) * gate[hs]
        for h in range(DN_HEADS):
            o_ref[0, c * rows:(c + 1) * rows, h * DN_DV:(h + 1) * DN_DV] = o[h]


def _deltanet(proj3, s0, buf0, convw, alog_row, dtb_row, norm_g, rows, chunks):
    b, t, _ = proj3.shape
    span = rows * chunks
    nc = t // span
    rows, rows_per_chunk = span, rows
    return pl.pallas_call(
        functools.partial(_deltanet_kernel, rows=rows_per_chunk, chunks=chunks),
        grid=(b, nc),
        in_specs=[
            pl.BlockSpec((1, rows, DN_CONV_CH), lambda i, c: (i, c, 0)),
            pl.BlockSpec((1, rows, DN_V), lambda i, c: (i, c, DN_CONV_CH // DN_V)),
            pl.BlockSpec((1, rows, LANES), lambda i, c: (i, c, BD_BLOCK)),
            pl.BlockSpec((1, DN_HEADS, DN_DK, DN_DV), lambda i, c: (i, 0, 0, 0)),
            pl.BlockSpec((1, CONV_W - 1, DN_CONV_CH), lambda i, c: (i, 0, 0)),
            pl.BlockSpec((CONV_W, DN_CONV_CH), lambda i, c: (0, 0)),
            pl.BlockSpec((1, LANES), lambda i, c: (0, 0)),
            pl.BlockSpec((1, LANES), lambda i, c: (0, 0)),
            pl.BlockSpec((1, DN_DV), lambda i, c: (0, 0)),
        ],
        out_specs=[
            pl.BlockSpec((1, rows, DN_V), lambda i, c: (i, c, 0)),
            pl.BlockSpec((1, DN_HEADS, DN_DK, DN_DV), lambda i, c: (i, 0, 0, 0)),
            pl.BlockSpec((1, CONV_W - 1, DN_CONV_CH), lambda i, c: (i, 0, 0)),
        ],
        out_shape=[
            jax.ShapeDtypeStruct((b, t, DN_V), F32),
            jax.ShapeDtypeStruct((b, DN_HEADS, DN_DK, DN_DV), F32),
            jax.ShapeDtypeStruct((b, CONV_W - 1, DN_CONV_CH), F32),
        ],
        scratch_shapes=[pltpu.VMEM((SUBLANES + rows, DN_CONV_CH), F32)],
        compiler_params=_params("arbitrary", "arbitrary"),
        name="deltanet",
    )(proj3, proj3, proj3, s0, buf0, convw, alog_row, dtb_row, norm_g)


def _rglru_kernel(lx_ref, ly_ref, h0_ref, buf0_ref, convw_ref, convb_ref, wr_ref, br_ref,
                  wi_ref, bi_ref, lam_ref, o_ref, h_ref, buf_ref, xpad_ref, a_ref, b_ref, *, rows):
    c = pl.program_id(1)

    @pl.when(c == 0)
    def _():
        xpad_ref[TAIL_ROW:SUBLANES, :] = buf0_ref[0]
        h_ref[...] = h0_ref[...]

    y, tail = _causal_conv(xpad_ref, lx_ref[0], convw_ref[...], rows)
    buf_ref[0] = tail
    x = y + convb_ref[...]
    xb = x.astype(BF16)
    sp = LRU_C * _softplus(-lam_ref[...])
    for n in range(LRU_BLOCKS):
        lo = n * LRU_BDIM
        xn = xb[:, lo:lo + LRU_BDIM]
        r = _sigmoid(_dot(xn, wr_ref[n]) + br_ref[:, lo:lo + LRU_BDIM])
        i = _sigmoid(_dot(xn, wi_ref[n]) + bi_ref[:, lo:lo + LRU_BDIM])
        log_a = -(r * sp[:, lo:lo + LRU_BDIM])
        a = jnp.exp(log_a)
        a_ref[:, lo:lo + LRU_BDIM] = a
        one_minus_a2 = -jnp.tanh(log_a) * (a * a + 1.0)
        b_ref[:, lo:lo + LRU_BDIM] = jnp.sqrt(one_minus_a2) * (i * x[:, lo:lo + LRU_BDIM])

    groups = rows // SUBLANES
    a = a_ref[...].reshape(groups, SUBLANES, LRU_WIDTH)
    b = b_ref[...].reshape(groups, SUBLANES, LRU_WIDTH)
    sub = lax.broadcasted_iota(jnp.int32, (groups, SUBLANES, LRU_WIDTH), 1)
    d = 1
    while d < SUBLANES:
        keep = sub >= d
        a_prev = jnp.where(keep, pltpu.roll(a, d, 1), 1.0)
        b_prev = jnp.where(keep, pltpu.roll(b, d, 1), 0.0)
        b = a * b_prev + b
        a = a * a_prev
        d *= 2
    a_ref[...] = a.reshape(rows, LRU_WIDTH)
    b_ref[...] = b.reshape(rows, LRU_WIDTH)

    def group_step(g, h):
        r0 = pl.multiple_of(g * SUBLANES, SUBLANES)
        hs = a_ref[pl.ds(r0, SUBLANES), :] * h + b_ref[pl.ds(r0, SUBLANES), :]
        yl = ly_ref[0, pl.ds(r0, SUBLANES), :]
        o_ref[0, pl.ds(r0, SUBLANES), :] = hs * jax.nn.gelu(yl, approximate=True)
        return hs[SUBLANES - 1:SUBLANES, :]

    h_ref[0] = lax.fori_loop(0, groups, group_step, h_ref[0])


def _rglru(proj3, h0, buf0, convw, convb, w_r, b_r, w_i, b_i, lam, rows):
    b, t, _ = proj3.shape
    nc = t // rows
    row = lambda i, c: (0, 0)
    return pl.pallas_call(
        functools.partial(_rglru_kernel, rows=rows),
        grid=(b, nc),
        in_specs=[
            pl.BlockSpec((1, rows, LRU_WIDTH), lambda i, c: (i, c, 4)),
            pl.BlockSpec((1, rows, LRU_WIDTH), lambda i, c: (i, c, 5)),
            pl.BlockSpec((1, 1, LRU_WIDTH), lambda i, c: (i, 0, 0)),
            pl.BlockSpec((1, CONV_W - 1, LRU_WIDTH), lambda i, c: (i, 0, 0)),
            pl.BlockSpec((CONV_W, LRU_WIDTH), row),
            pl.BlockSpec((1, LRU_WIDTH), row),
            pl.BlockSpec((LRU_BLOCKS, LRU_BDIM, LRU_BDIM), lambda i, c: (0, 0, 0)),
            pl.BlockSpec((1, LRU_WIDTH), row),
            pl.BlockSpec((LRU_BLOCKS, LRU_BDIM, LRU_BDIM), lambda i, c: (0, 0, 0)),
            pl.BlockSpec((1, LRU_WIDTH), row),
            pl.BlockSpec((1, LRU_WIDTH), row),
        ],
        out_specs=[
            pl.BlockSpec((1, rows, LRU_WIDTH), lambda i, c: (i, c, 0)),
            pl.BlockSpec((1, 1, LRU_WIDTH), lambda i, c: (i, 0, 0)),
            pl.BlockSpec((1, CONV_W - 1, LRU_WIDTH), lambda i, c: (i, 0, 0)),
        ],
        out_shape=[
            jax.ShapeDtypeStruct((b, t, LRU_WIDTH), F32),
            jax.ShapeDtypeStruct((b, 1, LRU_WIDTH), F32),
            jax.ShapeDtypeStruct((b, CONV_W - 1, LRU_WIDTH), F32),
        ],
        scratch_shapes=[
            pltpu.VMEM((SUBLANES + rows, LRU_WIDTH), F32),
            pltpu.VMEM((rows, LRU_WIDTH), F32),
            pltpu.VMEM((rows, LRU_WIDTH), F32),
        ],
        compiler_params=_params("arbitrary", "arbitrary"),
        name="rglru",
    )(proj3, proj3, h0, buf0, convw, convb, w_r, b_r, w_i, b_i, lam)


def _merge_kernel(x_ref, oa_ref, ob_ref, ga_ref, gb_ref, wa_ref, wb_ref, wo_ref, x1_ref):
    ya = _dot(oa_ref[...].astype(BF16), wa_ref[...])
    yb = _dot(ob_ref[...].astype(BF16), wb_ref[...])
    mix = _sigmoid(ga_ref[...]) * ya + _sigmoid(gb_ref[...]) * yb
    x1_ref[...] = x_ref[...] + _dot(mix.astype(BF16), wo_ref[...])


def _merge(x, o_a, o_b, proj, w_a, w_b, w_o, tm):
    n_tok = x.shape[0]
    tok = lambda i: (i, 0)
    full = lambda i: (0, 0)
    return pl.pallas_call(
        _merge_kernel,
        grid=(n_tok // tm,),
        in_specs=[
            pl.BlockSpec((tm, D_MODEL), tok),
            pl.BlockSpec((tm, DN_V), tok),
            pl.BlockSpec((tm, LRU_WIDTH), tok),
            pl.BlockSpec((tm, D_MODEL), lambda i: (i, 6)),
            pl.BlockSpec((tm, D_MODEL), lambda i: (i, 7)),
            pl.BlockSpec((DN_V, D_MODEL), full),
            pl.BlockSpec((LRU_WIDTH, D_MODEL), full),
            pl.BlockSpec((D_MODEL, D_MODEL), full),
        ],
        out_specs=pl.BlockSpec((tm, D_MODEL), tok),
        out_shape=jax.ShapeDtypeStruct((n_tok, D_MODEL), F32),
        compiler_params=_params("arbitrary"),
        name="merge",
    )(x, o_a, o_b, proj, proj, w_a, w_b, w_o)


def _candidate_slabs():
    return (
        ("row", 0, 0, 16, 0), ("row", 1, 0, 8, 0), ("row", 2, 0, 8, 0), ("row", 3, 0, 8, 0),
        ("col", 0, 8, 8, 0), ("col", 0, 0, 8, 4), ("col", 1, 0, 8, 4), ("col", 2, 0, 8, 4),
    )


def _candidate_positions():
    pos = []
    for kind, fixed, start, count, first_valid in _candidate_slabs():
        for r in range(start, start + count):
            i, j = (fixed, r) if kind == "row" else (r, fixed)
            masked = r - start < first_valid
            pos.append(PEER_TOPK * PEER_TOPK + len(pos) if masked else i * PEER_TOPK + j)
    assert len(set(pos)) == len(pos)
    return np.asarray(pos, np.float32).reshape(-1, 1)


MARK_BASE = -(2.0 ** 127 + 2.0 ** 126)
MARK_STEP = 2.0 ** 120
ORDER_BIG = float(1 << 20)


def _mark_top16(s, order):
    vals = []
    for r in range(PEER_TOPK):
        m = jnp.max(s, axis=1, keepdims=True)
        first = jnp.min(jnp.where(s == m, order, ORDER_BIG), axis=1, keepdims=True)
        s = jnp.where(order == first, MARK_BASE - r * MARK_STEP, s)
        vals.append(m)
    return s, jnp.concatenate(vals, axis=1)


def _is_marked(x):
    return (x <= MARK_BASE) & (x > MARK_BASE - PEER_TOPK * MARK_STEP)


def _rank_plus_half(marked):
    return jnp.where(_is_marked(marked), (MARK_BASE - marked) * (1.0 / MARK_STEP), float(PEER_TOPK)) + 0.5


def _retrieve_kernel(x1_ref, g_ref, wqh_ref, wql_ref, kblk_ref, pos_ref,
                     n2_ref, lena_ref, ca_ref, rankb_ref, pb_ref,
                     qt_ref, st_ref, v_ref, rank1_ref, *, tm):
    x1 = x1_ref[...]
    n2 = (x1 * _rms_scale(x1)) * g_ref[...]
    nh = n2.astype(BF16)
    nl = (n2 - nh.astype(F32)).astype(BF16)
    n2_ref[...] = nh
    nt = ((1,), (1,))
    qt_ref[...] = _dot(wqh_ref[...], nh, nt) + (_dot(wqh_ref[...], nl, nt) + _dot(wql_ref[...], nh, nt))

    key_iota = lax.broadcasted_iota(jnp.int32, (1, PEER_NKEYS, LANES), 1).astype(F32)
    pos = jnp.broadcast_to(pos_ref[...], (pos_ref.shape[0], LANES))[None]

    def head(h, _):
        q = qt_ref[pl.ds(pl.multiple_of(h * PEER_DK, PEER_DK), PEER_DK), :]
        st_ref[h] = _dot3(kblk_ref[h], q).reshape(2, PEER_NKEYS, tm)
        for lb in range(tm // LANES):
            lanes = slice(lb * LANES, (lb + 1) * LANES)
            s = st_ref[h, :, :, lanes]
            marked, vals = _mark_top16(s, key_iota)
            v_ref[h, :, :, lanes] = vals
            rank1_ref[h, :, lanes] = marked[0]
            rankb_ref[h, :, lanes] = _rank_plus_half(marked[1]).astype(BF16)
            pb_ref[h, :, lanes] = jnp.exp(s[1] - vals[1, 0:1]).astype(BF16)
        return 0

    lax.fori_loop(0, PEER_HEADS, head, 0)

    for lb in range(tm // LANES):
        lanes = slice(lb * LANES, (lb + 1) * LANES)
        v1 = v_ref[:, 0, :, lanes]
        v2 = v_ref[:, 1, :, lanes]
        pieces = []
        for kind, fixed, start, count, first_valid in _candidate_slabs():
            if kind == "row":
                piece = v1[:, fixed:fixed + 1] + v2[:, start:start + count]
            else:
                piece = v1[:, start:start + count] + v2[:, fixed:fixed + 1]
            if first_valid:
                local = lax.broadcasted_iota(jnp.int32, piece.shape, 1)
                piece = jnp.where(local >= first_valid, piece, NEG_INF)
            pieces.append(piece)
        cand = jnp.concatenate(pieces, axis=1)
        picked = _is_marked(_mark_top16(cand, pos)[0])
        t0 = v1[:, 0:1] + v2[:, 0:1]
        z = jnp.sum(jnp.where(picked, jnp.exp(cand - t0), 0.0), axis=1, keepdims=True)
        count_i = [None] * PEER_TOPK
        off = 0
        for kind, fixed, start, count, first_valid in _candidate_slabs():
            hit = jnp.where(picked[:, off:off + count], 1.0, 0.0)
            if kind == "row":
                add = [(fixed, jnp.sum(hit, axis=1, keepdims=True))]
            else:
                add = [(start + r, hit[:, r:r + 1]) for r in range(first_valid, count)]
            for i, term in add:
                count_i[i] = term if count_i[i] is None else count_i[i] + term
            off += count
        for h in range(PEER_HEADS):
            marked1 = rank1_ref[h, :, lanes]
            len_a = jnp.zeros((PEER_NKEYS, LANES), F32)
            for i in range(PEER_TOPK):
                len_a = jnp.where(marked1 == MARK_BASE - i * MARK_STEP, count_i[i][h], len_a)
            lena_ref[h, :, lanes] = len_a
            ca_ref[h, :, lanes] = jnp.exp(st_ref[h, 0, :, lanes] - v1[h, 0:1]) / z[h]


def _retrieve(x1, g2, wq_hi, wq_lo, kblk, tm):
    n_tok = x1.shape[0]
    pos = jnp.asarray(_candidate_positions())
    key_spec = pl.BlockSpec((PEER_HEADS, PEER_NKEYS, tm), lambda i: (0, 0, i))
    per_key = lambda dtype: jax.ShapeDtypeStruct((PEER_HEADS, PEER_NKEYS, n_tok), dtype)
    return pl.pallas_call(
        functools.partial(_retrieve_kernel, tm=tm),
        grid=(n_tok // tm,),
        in_specs=[
            pl.BlockSpec((tm, D_MODEL), lambda i: (i, 0)),
            pl.BlockSpec((1, D_MODEL), lambda i: (0, 0)),
            pl.BlockSpec((D_MODEL, D_MODEL), lambda i: (0, 0)),
            pl.BlockSpec((D_MODEL, D_MODEL), lambda i: (0, 0)),
            pl.BlockSpec((PEER_HEADS, 2 * PEER_NKEYS, PEER_DK), lambda i: (0, 0, 0)),
            pl.BlockSpec(pos.shape, lambda i: (0, 0)),
        ],
        out_specs=[pl.BlockSpec((tm, D_MODEL), lambda i: (i, 0)), key_spec, key_spec, key_spec, key_spec],
        out_shape=[jax.ShapeDtypeStruct((n_tok, D_MODEL), BF16), per_key(F32), per_key(F32),
                   per_key(BF16), per_key(BF16)],
        scratch_shapes=[
            pltpu.VMEM((D_MODEL, tm), F32),
            pltpu.VMEM((PEER_HEADS, 2, PEER_NKEYS, tm), F32),
            pltpu.VMEM((PEER_HEADS, 2, PEER_TOPK, tm), F32),
            pltpu.VMEM((PEER_HEADS, PEER_NKEYS, tm), F32),
        ],
        compiler_params=_params("arbitrary"),
        name="retrieve",
    )(x1, g2, wq_hi, wq_lo, kblk, pos)


EXPERT_TILE = 1024
A_PER_TILE = EXPERT_TILE // PEER_NKEYS


def _experts_kernel(x1_ref, n2_ref, u_ref, vt_ref, lena_ref, ca_ref, rankb_ref, pb_ref, fg_ref,
                    y_ref, acc_ref, *, tm):
    j = pl.program_id(1)

    @pl.when(j == 0)
    def _():
        acc_ref[...] = jnp.zeros_like(acc_ref)

    s = _dot(u_ref[...], n2_ref[...], ((1,), (1,)))
    pack = 2 * SUBLANES
    groups = PEER_NKEYS // pack
    acts = []
    for al in range(A_PER_TILE):
        w = None
        for h in range(PEER_HEADS):
            len_a = jnp.broadcast_to(lena_ref[h, al:al + 1, :], (pack, tm)).astype(BF16)
            c_a = jnp.broadcast_to(ca_ref[h, al:al + 1, :], (pack, tm)).astype(BF16)
            rank_b = rankb_ref[h].reshape(groups, pack, tm)
            p_b = pb_ref[h].reshape(groups, pack, tm)
            t = jnp.where(rank_b < len_a[None], p_b, jnp.zeros_like(p_b)) * c_a[None]
            w = t if w is None else w + t
        sa = s[al * PEER_NKEYS:(al + 1) * PEER_NKEYS]
        act = 0.5 * sa * (1.0 + lax.erf(sa * (1.0 / math.sqrt(2.0))))
        acts.append(act.astype(BF16) * w.reshape(PEER_NKEYS, tm))
    acc_ref[...] += _dot(vt_ref[...], jnp.concatenate(acts, axis=0))

    @pl.when(j == pl.num_programs(1) - 1)
    def _():
        x2 = x1_ref[...] + acc_ref[...].T
        y_ref[...] = (x2 * _rms_scale(x2)) * fg_ref[...]


def _experts(x1, n2, u_bf16, vt_bf16, len_a, c_a, rank_b, p_b, final_g, tm):
    n_tok = x1.shape[0]
    n_exp = u_bf16.shape[0]
    a_spec = pl.BlockSpec((PEER_HEADS, A_PER_TILE, tm), lambda i, j: (0, j, i))
    b_spec = pl.BlockSpec((PEER_HEADS, PEER_NKEYS, tm), lambda i, j: (0, 0, i))
    return pl.pallas_call(
        functools.partial(_experts_kernel, tm=tm),
        grid=(n_tok // tm, n_exp // EXPERT_TILE),
        in_specs=[
            pl.BlockSpec((tm, D_MODEL), lambda i, j: (i, 0)),
            pl.BlockSpec((tm, D_MODEL), lambda i, j: (i, 0)),
            pl.BlockSpec((EXPERT_TILE, D_MODEL), lambda i, j: (j, 0)),
            pl.BlockSpec((D_MODEL, EXPERT_TILE), lambda i, j: (0, j)),
            a_spec, a_spec, b_spec, b_spec,
            pl.BlockSpec((1, D_MODEL), lambda i, j: (0, 0)),
        ],
        out_specs=pl.BlockSpec((tm, D_MODEL), lambda i, j: (i, 0)),
        out_shape=jax.ShapeDtypeStruct((n_tok, D_MODEL), F32),
        scratch_shapes=[pltpu.VMEM((D_MODEL, tm), F32)],
        compiler_params=_params("arbitrary", "arbitrary"),
        name="experts",
    )(x1, n2, u_bf16, vt_bf16, len_a, c_a, rank_b, p_b, final_g)


def _token_tile(n_tok, cap):
    tm = cap
    while n_tok % tm:
        tm //= 2
    return tm


def _stream(x, s0, dn_buf0, h0, lru_buf0, p):
    b, t, _ = x.shape
    n_tok = b * t
    rows = min(CHUNK, t)
    xf = x.reshape(n_tok, D_MODEL)
    proj = _in_proj(xf, p["ln1_g"], p["w_in"], _token_tile(n_tok, 1024))
    proj3 = proj.reshape(b, t, PROJ_COLS)
    o_a, s_new, dn_buf = _deltanet(proj3, s0, dn_buf0, p["dn_conv_w"], p["alog_row"], p["dtb_row"],
                                   p["dn_norm_g"], rows, min(DN_CHUNKS_PER_STEP, t // rows))
    lru_rows = _token_tile(t, 256)
    o_b, h_new, lru_buf = _rglru(proj3, h0.reshape(b, 1, LRU_WIDTH), lru_buf0, p["lru_conv_w"],
                                 p["lru_conv_b"], p["lru_w_r"], p["lru_b_r"], p["lru_w_i"],
                                 p["lru_b_i"], p["lru_lambda"], lru_rows)
    x1 = _merge(xf, o_a.reshape(n_tok, DN_V), o_b.reshape(n_tok, LRU_WIDTH), proj,
                p["w_branch_a"], p["w_branch_b"], p["w_out"], _token_tile(n_tok, 512))
    tr = _token_tile(n_tok, 256)
    n2, len_a, c_a, rank_b, p_b = _retrieve(x1, p["ln2_g"], p["wq_hi"], p["wq_lo"], p["kblk"], tr)
    y = _experts(x1, n2, p["peer_u"], p["peer_vt"], len_a, c_a, rank_b, p_b, p["final_g"],
                 _token_tile(n_tok, 512))
    return (y.reshape(b, t, D_MODEL), s_new, dn_buf, h_new.reshape(b, LRU_WIDTH), lru_buf)


def _prepare(w_in, dn_conv_w, dn_a_log, dn_dt_bias, dn_norm_g, lru_conv_w, lru_conv_b, lru_w_r,
             lru_b_r, lru_w_i, lru_b_i, lru_lambda, w_branch_a, w_branch_b, w_out, ln1_g, ln2_g,
             peer_w_q, peer_keys, peer_u, peer_v, final_g):
    row = lambda v: v.reshape(1, -1).astype(F32)
    lane_row = lambda v: jnp.zeros((1, LANES), F32).at[0, DN_HEADS:2 * DN_HEADS].set(v)
    w_in_p = jnp.pad(w_in, ((0, 0), (0, PROJ_COLS - w_in.shape[1]))).astype(BF16)
    wq_t = peer_w_q.T
    wq_hi = wq_t.astype(BF16)
    wq_lo = (wq_t - wq_hi.astype(F32)).astype(BF16)
    zeros = jnp.zeros((PEER_HEADS, PEER_NKEYS, PEER_DKH), F32)
    kblk = jnp.concatenate([
        jnp.concatenate([peer_keys[:, 0], zeros], axis=-1),
        jnp.concatenate([zeros, peer_keys[:, 1]], axis=-1)], axis=1)
    return dict(
        w_in=w_in_p, dn_conv_w=dn_conv_w, alog_row=lane_row(dn_a_log), dtb_row=lane_row(dn_dt_bias),
        dn_norm_g=row(dn_norm_g), lru_conv_w=lru_conv_w, lru_conv_b=row(lru_conv_b),
        lru_w_r=lru_w_r.astype(BF16), lru_b_r=row(lru_b_r), lru_w_i=lru_w_i.astype(BF16),
        lru_b_i=row(lru_b_i), lru_lambda=row(lru_lambda), w_branch_a=w_branch_a.astype(BF16),
        w_branch_b=w_branch_b.astype(BF16), w_out=w_out.astype(BF16), ln1_g=row(ln1_g),
        ln2_g=row(ln2_g), wq_hi=wq_hi, wq_lo=wq_lo, kblk=kblk, peer_u=peer_u.astype(BF16),
        peer_vt=peer_v.T.astype(BF16), final_g=row(final_g))


def kernel(x_prompt, x_sample, state_dn, state_dn_conv, state_lru_h, state_lru_conv, w_in, dn_conv_w, dn_a_log, dn_dt_bias, dn_norm_g, lru_conv_w, lru_conv_b, lru_w_r, lru_b_r, lru_w_i, lru_b_i, lru_lambda, w_branch_a, w_branch_b, w_out, ln1_g, ln2_g, peer_w_q, peer_keys, peer_u, peer_v, final_g):
    depth = w_in.shape[0]
    assert depth == 1, "the final norm is fused into the layer's last kernel"
    bp = x_prompt.shape[0]
    dt = x_prompt.dtype
    p = _prepare(w_in[0], dn_conv_w[0], dn_a_log[0], dn_dt_bias[0], dn_norm_g[0], lru_conv_w[0],
                 lru_conv_b[0], lru_w_r[0], lru_b_r[0], lru_w_i[0], lru_b_i[0], lru_lambda[0],
                 w_branch_a[0], w_branch_b[0], w_out[0], ln1_g[0], ln2_g[0], peer_w_q[0],
                 peer_keys[0], peer_u[0], peer_v[0], final_g)
    prompt = _stream(
        x_prompt,
        jnp.zeros((bp, DN_HEADS, DN_DK, DN_DV), dt),
        jnp.zeros((bp, CONV_W - 1, DN_CONV_CH), dt),
        jnp.zeros((bp, LRU_WIDTH), dt),
        jnp.zeros((bp, CONV_W - 1, LRU_WIDTH), dt),
        p)
    sample = _stream(x_sample, state_dn[0], state_dn_conv[0], state_lru_h[0], state_lru_conv[0], p)
    y_p, s_p, db_p, h_p, lb_p = prompt
    y_s, s_s, db_s, h_s, lb_s = sample
    return (y_p, y_s, s_p[None], db_p[None], h_p[None], lb_p[None],
            s_s[None], db_s[None], h_s[None], lb_s[None])
```

```python
import functools
import math

import jax
import jax.numpy as jnp
import numpy as np
from jax import lax
from jax.experimental import pallas as pl
from jax.experimental.pallas import tpu as pltpu

F32 = jnp.float32
BF16 = jnp.bfloat16

D_MODEL = 1024
CONV_W = 4
EPS = 1e-6
DN_HEADS = 8
DN_DK = 128
DN_DV = 128
DN_QK = DN_HEADS * DN_DK
DN_V = DN_HEADS * DN_DV
DN_CONV_CH = 2 * DN_QK + DN_V
LRU_WIDTH = D_MODEL
LRU_BLOCKS = 8
LRU_BDIM = LRU_WIDTH // LRU_BLOCKS
LRU_C = 8.0
PEER_HEADS = 8
PEER_NKEYS = 128
PEER_DK = 128
PEER_DKH = PEER_DK // 2
PEER_TOPK = 16
CHUNK = 64

LANES = 128
SUBLANES = 8
MAIN_COLS = 8192
PROJ_COLS = MAIN_COLS + LANES
PROJ_COL_TILE = PROJ_COLS // 5
BD_BLOCK = MAIN_COLS // LANES
TAIL_ROW = SUBLANES - (CONV_W - 1)
VMEM_LIMIT = 56 * 1024 * 1024
NEG_INF = float("-inf")


def _params(*sem):
    return pltpu.CompilerParams(dimension_semantics=sem, vmem_limit_bytes=VMEM_LIMIT)


def _split3(x):
    a = x.astype(BF16)
    r = x - a.astype(F32)
    b = r.astype(BF16)
    c = (r - b.astype(F32)).astype(BF16)
    return a, b, c


def _dot(a, b, dims=None):
    if dims is None:
        return jnp.dot(a, b, preferred_element_type=F32)
    return lax.dot_general(a, b, (dims, ((), ())), preferred_element_type=F32)


def _dot3(a, b, dims=None):
    ah = a.astype(BF16)
    al = (a - ah.astype(F32)).astype(BF16)
    bh = b.astype(BF16)
    bl = (b - bh.astype(F32)).astype(BF16)
    return _dot(ah, bh, dims) + (_dot(ah, bl, dims) + _dot(al, bh, dims))


def _dot_exact_lhs(a_bf16, b, dims=None):
    b0, b1, b2 = _split3(b)
    return _dot(a_bf16, b0, dims) + (_dot(a_bf16, b1, dims) + _dot(a_bf16, b2, dims))


def _sigmoid(x):
    return 1.0 / (1.0 + jnp.exp(-x))


def _softplus(x):
    return jnp.maximum(x, 0.0) + jnp.log1p(jnp.exp(-jnp.abs(x)))


def _rms_scale(x):
    return lax.rsqrt(jnp.mean(x * x, axis=-1, keepdims=True) + EPS)


def _causal_conv(xpad_ref, new_rows, w, rows):
    xpad_ref[SUBLANES:SUBLANES + rows, :] = new_rows
    y = xpad_ref[TAIL_ROW:TAIL_ROW + rows, :] * w[0:1, :]
    for j in range(1, CONV_W):
        y = y + xpad_ref[TAIL_ROW + j:TAIL_ROW + j + rows, :] * w[j:j + 1, :]
    tail = xpad_ref[rows + TAIL_ROW:rows + SUBLANES, :]
    xpad_ref[TAIL_ROW:SUBLANES, :] = tail
    return y, tail


def _inproj_kernel(x_ref, g_ref, w_ref, o_ref):
    x = x_ref[...]
    n = (x * _rms_scale(x)) * g_ref[...]
    o_ref[...] = _dot(n.astype(BF16), w_ref[...])


def _in_proj(x, g, w_bf16, tm):
    n_tok = x.shape[0]
    return pl.pallas_call(
        _inproj_kernel,
        grid=(PROJ_COLS // PROJ_COL_TILE, n_tok // tm),
        in_specs=[
            pl.BlockSpec((tm, D_MODEL), lambda j, i: (i, 0)),
            pl.BlockSpec((1, D_MODEL), lambda j, i: (0, 0)),
            pl.BlockSpec((D_MODEL, PROJ_COL_TILE), lambda j, i: (0, j)),
        ],
        out_specs=pl.BlockSpec((tm, PROJ_COL_TILE), lambda j, i: (i, j)),
        out_shape=jax.ShapeDtypeStruct((n_tok, PROJ_COLS), F32),
        compiler_params=_params("arbitrary", "arbitrary"),
        name="in_proj",
    )(x, g, w_bf16)


def _bdot(a, b, eq):
    return jnp.einsum(eq, a, b, preferred_element_type=F32)


def _bdot1(a, b, eq):
    return _bdot(a.astype(BF16), b.astype(BF16), eq)


def _bdot3(a, b, eq):
    ah = a.astype(BF16)
    al = (a - ah.astype(F32)).astype(BF16)
    bh = b.astype(BF16)
    bl = (b - bh.astype(F32)).astype(BF16)
    return _bdot(ah, bh, eq) + (_bdot(ah, bl, eq) + _bdot(al, bh, eq))


MM = "gij,gjk->gik"
DN_CHUNKS_PER_STEP = 4


def _unit_lower_inverse(nmat, rows, row_i, col_i):
    eye = jnp.where(row_i == col_i, 1.0, 0.0)
    blk = SUBLANES
    differ = row_i ^ col_i
    nd = jnp.where(differ < blk, nmat, 0.0)
    t = eye - nd
    m = _bdot3(nd, nd, MM)
    t = t + _bdot3(t, m, MM)
    m = _bdot3(m, m, MM)
    t = t + _bdot3(t, m, MM)
    while blk < rows:
        off = jnp.where((differ >> int(math.log2(blk))) == 1, nmat, 0.0)
        t = t - _bdot3(t, _bdot3(off, t, MM), MM)
        blk *= 2
    return t


def _deltanet_kernel(qkv_ref, z_ref, bd_ref, s0_ref, buf0_ref, convw_ref, alog_ref, dtb_ref,
                     ng_ref, o_ref, s_ref, buf_ref, xpad_ref, *, rows, chunks):
    span = rows * chunks

    @pl.when(pl.program_id(1) == 0)
    def _():
        xpad_ref[TAIL_ROW:SUBLANES, :] = buf0_ref[0]
        s_ref[...] = s0_ref[...]

    y, tail = _causal_conv(xpad_ref, qkv_ref[0], convw_ref[...], span)
    buf_ref[0] = tail
    act = y * _sigmoid(y)

    bd = bd_ref[0]
    beta_all = _sigmoid(bd)
    g_all = -jnp.exp(alog_ref[...]) * _softplus(bd + dtb_ref[...])
    t_row = lax.broadcasted_iota(jnp.int32, (span, span), 0)
    t_col = lax.broadcasted_iota(jnp.int32, (span, span), 1)
    tri = jnp.where((t_row ^ t_col) < rows, jnp.where(t_row >= t_col, 1.0, 0.0), 0.0).astype(BF16)
    gcum = _dot_exact_lhs(tri, g_all)
    pad = -span % LANES
    gcum_t = (jnp.concatenate([gcum, jnp.zeros((pad, LANES), F32)], axis=0) if pad else gcum).T

    def per_head(src, col0, width):
        return jnp.stack([src[c * rows:(c + 1) * rows, col0 + h * width:col0 + (h + 1) * width]
                          for c in range(chunks) for h in range(DN_HEADS)])

    q = per_head(act, 0, DN_DK)
    k = per_head(act, DN_QK, DN_DK)
    v = per_head(act, 2 * DN_QK, DN_DV)
    z = per_head(z_ref[0], 0, DN_DV)
    beta = per_head(beta_all, 0, 1)
    gc = per_head(gcum, DN_HEADS, 1)
    gr = jnp.stack([gcum_t[DN_HEADS + h:DN_HEADS + h + 1, c * rows:(c + 1) * rows]
                    for c in range(chunks) for h in range(DN_HEADS)])
    g_last = gc[:, rows - 1:rows, :]

    q = q * (lax.rsqrt(jnp.sum(q * q, axis=-1, keepdims=True) + EPS) * (DN_DK ** -0.5))
    k = k * lax.rsqrt(jnp.sum(k * k, axis=-1, keepdims=True) + EPS)
    row_i = lax.broadcasted_iota(jnp.int32, (rows, rows), 0)
    col_i = lax.broadcasted_iota(jnp.int32, (rows, rows), 1)
    incl = row_i >= col_i
    dec = jnp.where(incl, jnp.exp(jnp.where(incl, gc - gr, 0.0)), 0.0)
    kb = k.astype(BF16)
    kk = _bdot(kb, kb, "gik,gjk->gij")
    nmat = jnp.where(row_i > col_i, beta * kk * dec, 0.0)
    gam = jnp.exp(gc)
    rhs = jnp.concatenate([(beta * gam) * k, beta * v], axis=-1)
    sol = _bdot1(_unit_lower_inverse(nmat, rows, row_i, col_i), rhs, MM)
    uv = sol[:, :, DN_DK:]
    wq = jnp.concatenate([sol[:, :, :DN_DK], q * gam], axis=1)
    qk = _bdot(q.astype(BF16), kb, "gik,gjk->gij") * dec
    kend = k * jnp.exp(g_last - gc)
    gend = jnp.exp(g_last)
    gate = ng_ref[...] * (z * _sigmoid(z))

    for c in range(chunks):
        hs = slice(c * DN_HEADS, (c + 1) * DN_HEADS)
        s_old = s_ref[0]
        ws = _bdot1(wq[hs], s_old, MM)
        u = uv[hs] - ws[:, :rows]
        ub = u.astype(BF16)
        o = ws[:, rows:] + _bdot(qk[hs].astype(BF16), ub, MM)
        s_ref[0] = gend[hs] * s_old + _bdot(kend[hs].astype(BF16), ub, "glk,glv->gkv")
        o = (o * _rms_scale(o)) * gate[hs]
        for h in range(DN_HEADS):
            o_ref[0, c * rows:(c + 1) * rows, h * DN_DV:(h + 1) * DN_DV] = o[h]


def _deltanet(proj3, s0, buf0, convw, alog_row, dtb_row, norm_g, rows, chunks):
    b, t, _ = proj3.shape
    span = rows * chunks
    nc = t // span
    rows, rows_per_chunk = span, rows
    return pl.pallas_call(
        functools.partial(_deltanet_kernel, rows=rows_per_chunk, chunks=chunks),
        grid=(b, nc),
        in_specs=[
            pl.BlockSpec((1, rows, DN_CONV_CH), lambda i, c: (i, c, 0)),
            pl.BlockSpec((1, rows, DN_V), lambda i, c: (i, c, DN_CONV_CH // DN_V)),
            pl.BlockSpec((1, rows, LANES), lambda i, c: (i, c, BD_BLOCK)),
            pl.BlockSpec((1, DN_HEADS, DN_DK, DN_DV), lambda i, c: (i, 0, 0, 0)),
            pl.BlockSpec((1, CONV_W - 1, DN_CONV_CH), lambda i, c: (i, 0, 0)),
            pl.BlockSpec((CONV_W, DN_CONV_CH), lambda i, c: (0, 0)),
            pl.BlockSpec((1, LANES), lambda i, c: (0, 0)),
            pl.BlockSpec((1, LANES), lambda i, c: (0, 0)),
            pl.BlockSpec((1, DN_DV), lambda i, c: (0, 0)),
        ],
        out_specs=[
            pl.BlockSpec((1, rows, DN_V), lambda i, c: (i, c, 0)),
            pl.BlockSpec((1, DN_HEADS, DN_DK, DN_DV), lambda i, c: (i, 0, 0, 0)),
            pl.BlockSpec((1, CONV_W - 1, DN_CONV_CH), lambda i, c: (i, 0, 0)),
        ],
        out_shape=[
            jax.ShapeDtypeStruct((b, t, DN_V), F32),
            jax.ShapeDtypeStruct((b, DN_HEADS, DN_DK, DN_DV), F32),
            jax.ShapeDtypeStruct((b, CONV_W - 1, DN_CONV_CH), F32),
        ],
        scratch_shapes=[pltpu.VMEM((SUBLANES + rows, DN_CONV_CH), F32)],
        compiler_params=_params("arbitrary", "arbitrary"),
        name="deltanet",
    )(proj3, proj3, proj3, s0, buf0, convw, alog_row, dtb_row, norm_g)


def _rglru_kernel(lx_ref, ly_ref, h0_ref, buf0_ref, convw_ref, convb_ref, wr_ref, br_ref,
                  wi_ref, bi_ref, lam_ref, o_ref, h_ref, buf_ref, xpad_ref, a_ref, b_ref, *, rows):
    c = pl.program_id(1)

    @pl.when(c == 0)
    def _():
        xpad_ref[TAIL_ROW:SUBLANES, :] = buf0_ref[0]
        h_ref[...] = h0_ref[...]

    y, tail = _causal_conv(xpad_ref, lx_ref[0], convw_ref[...], rows)
    buf_ref[0] = tail
    x = y + convb_ref[...]
    xb = x.astype(BF16)
    sp = LRU_C * _softplus(-lam_ref[...])
    for n in range(LRU_BLOCKS):
        lo = n * LRU_BDIM
        xn = xb[:, lo:lo + LRU_BDIM]
        r = _sigmoid(_dot(xn, wr_ref[n]) + br_ref[:, lo:lo + LRU_BDIM])
        i = _sigmoid(_dot(xn, wi_ref[n]) + bi_ref[:, lo:lo + LRU_BDIM])
        log_a = -(r * sp[:, lo:lo + LRU_BDIM])
        a = jnp.exp(log_a)
        a_ref[:, lo:lo + LRU_BDIM] = a
        one_minus_a2 = -jnp.tanh(log_a) * (a * a + 1.0)
        b_ref[:, lo:lo + LRU_BDIM] = jnp.sqrt(one_minus_a2) * (i * x[:, lo:lo + LRU_BDIM])

    groups = rows // SUBLANES
    a = a_ref[...].reshape(groups, SUBLANES, LRU_WIDTH)
    b = b_ref[...].reshape(groups, SUBLANES, LRU_WIDTH)
    sub = lax.broadcasted_iota(jnp.int32, (groups, SUBLANES, LRU_WIDTH), 1)
    d = 1
    while d < SUBLANES:
        keep = sub >= d
        a_prev = jnp.where(keep, pltpu.roll(a, d, 1), 1.0)
        b_prev = jnp.where(keep, pltpu.roll(b, d, 1), 0.0)
        b = a * b_prev + b
        a = a * a_prev
        d *= 2
    a_ref[...] = a.reshape(rows, LRU_WIDTH)
    b_ref[...] = b.reshape(rows, LRU_WIDTH)

    def group_step(g, h):
        r0 = pl.multiple_of(g * SUBLANES, SUBLANES)
        hs = a_ref[pl.ds(r0, SUBLANES), :] * h + b_ref[pl.ds(r0, SUBLANES), :]
        yl = ly_ref[0, pl.ds(r0, SUBLANES), :]
        o_ref[0, pl.ds(r0, SUBLANES), :] = hs * jax.nn.gelu(yl, approximate=True)
        return hs[SUBLANES - 1:SUBLANES, :]

    h_ref[0] = lax.fori_loop(0, groups, group_step, h_ref[0])


def _rglru(proj3, h0, buf0, convw, convb, w_r, b_r, w_i, b_i, lam, rows):
    b, t, _ = proj3.shape
    nc = t // rows
    row = lambda i, c: (0, 0)
    return pl.pallas_call(
        functools.partial(_rglru_kernel, rows=rows),
        grid=(b, nc),
        in_specs=[
            pl.BlockSpec((1, rows, LRU_WIDTH), lambda i, c: (i, c, 4)),
            pl.BlockSpec((1, rows, LRU_WIDTH), lambda i, c: (i, c, 5)),
            pl.BlockSpec((1, 1, LRU_WIDTH), lambda i, c: (i, 0, 0)),
            pl.BlockSpec((1, CONV_W - 1, LRU_WIDTH), lambda i, c: (i, 0, 0)),
            pl.BlockSpec((CONV_W, LRU_WIDTH), row),
            pl.BlockSpec((1, LRU_WIDTH), row),
            pl.BlockSpec((LRU_BLOCKS, LRU_BDIM, LRU_BDIM), lambda i, c: (0, 0, 0)),
            pl.BlockSpec((1, LRU_WIDTH), row),
            pl.BlockSpec((LRU_BLOCKS, LRU_BDIM, LRU_BDIM), lambda i, c: (0, 0, 0)),
            pl.BlockSpec((1, LRU_WIDTH), row),
            pl.BlockSpec((1, LRU_WIDTH), row),
        ],
        out_specs=[
            pl.BlockSpec((1, rows, LRU_WIDTH), lambda i, c: (i, c, 0)),
            pl.BlockSpec((1, 1, LRU_WIDTH), lambda i, c: (i, 0, 0)),
            pl.BlockSpec((1, CONV_W - 1, LRU_WIDTH), lambda i, c: (i, 0, 0)),
        ],
        out_shape=[
            jax.ShapeDtypeStruct((b, t, LRU_WIDTH), F32),
            jax.ShapeDtypeStruct((b, 1, LRU_WIDTH), F32),
            jax.ShapeDtypeStruct((b, CONV_W - 1, LRU_WIDTH), F32),
        ],
        scratch_shapes=[
            pltpu.VMEM((SUBLANES + rows, LRU_WIDTH), F32),
            pltpu.VMEM((rows, LRU_WIDTH), F32),
            pltpu.VMEM((rows, LRU_WIDTH), F32),
        ],
        compiler_params=_params("arbitrary", "arbitrary"),
        name="rglru",
    )(proj3, proj3, h0, buf0, convw, convb, w_r, b_r, w_i, b_i, lam)


def _merge_kernel(x_ref, oa_ref, ob_ref, ga_ref, gb_ref, wa_ref, wb_ref, wo_ref, x1_ref):
    ya = _dot(oa_ref[...].astype(BF16), wa_ref[...])
    yb = _dot(ob_ref[...].astype(BF16), wb_ref[...])
    mix = _sigmoid(ga_ref[...]) * ya + _sigmoid(gb_ref[...]) * yb
    x1_ref[...] = x_ref[...] + _dot(mix.astype(BF16), wo_ref[...])


def _merge(x, o_a, o_b, proj, w_a, w_b, w_o, tm):
    n_tok = x.shape[0]
    tok = lambda i: (i, 0)
    full = lambda i: (0, 0)
    return pl.pallas_call(
        _merge_kernel,
        grid=(n_tok // tm,),
        in_specs=[
            pl.BlockSpec((tm, D_MODEL), tok),
            pl.BlockSpec((tm, DN_V), tok),
            pl.BlockSpec((tm, LRU_WIDTH), tok),
            pl.BlockSpec((tm, D_MODEL), lambda i: (i, 6)),
            pl.BlockSpec((tm, D_MODEL), lambda i: (i, 7)),
            pl.BlockSpec((DN_V, D_MODEL), full),
            pl.BlockSpec((LRU_WIDTH, D_MODEL), full),
            pl.BlockSpec((D_MODEL, D_MODEL), full),
        ],
        out_specs=pl.BlockSpec((tm, D_MODEL), tok),
        out_shape=jax.ShapeDtypeStruct((n_tok, D_MODEL), F32),
        compiler_params=_params("arbitrary"),
        name="merge",
    )(x, o_a, o_b, proj, proj, w_a, w_b, w_o)


def _candidate_slabs():
    return (
        ("row", 0, 0, 16, 0), ("row", 1, 0, 8, 0), ("row", 2, 0, 8, 0), ("row", 3, 0, 8, 0),
        ("col", 0, 8, 8, 0), ("col", 0, 0, 8, 4), ("col", 1, 0, 8, 4), ("col", 2, 0, 8, 4),
    )


def _candidate_positions():
    pos = []
    for kind, fixed, start, count, first_valid in _candidate_slabs():
        for r in range(start, start + count):
            i, j = (fixed, r) if kind == "row" else (r, fixed)
            masked = r - start < first_valid
            pos.append(PEER_TOPK * PEER_TOPK + len(pos) if masked else i * PEER_TOPK + j)
    assert len(set(pos)) == len(pos)
    return np.asarray(pos, np.float32).reshape(-1, 1)


MARK_BASE = -(2.0 ** 127 + 2.0 ** 126)
MARK_STEP = 2.0 ** 120
ORDER_BIG = float(1 << 20)


def _mark_top16(s, order):
    vals = []
    for r in range(PEER_TOPK):
        m = jnp.max(s, axis=1, keepdims=True)
        first = jnp.min(jnp.where(s == m, order, ORDER_BIG), axis=1, keepdims=True)
        s = jnp.where(order == first, MARK_BASE - r * MARK_STEP, s)
        vals.append(m)
    return s, jnp.concatenate(vals, axis=1)


def _is_marked(x):
    return (x <= MARK_BASE) & (x > MARK_BASE - PEER_TOPK * MARK_STEP)


def _rank_plus_half(marked):
    return jnp.where(_is_marked(marked), (MARK_BASE - marked) * (1.0 / MARK_STEP), float(PEER_TOPK)) + 0.5


def _retrieve_kernel(x1_ref, g_ref, wqh_ref, wql_ref, kblk_ref, pos_ref,
                     n2_ref, lena_ref, ca_ref, rankb_ref, pb_ref,
                     qt_ref, st_ref, v_ref, mark1_ref, *, tm):
    x1 = x1_ref[...]
    n2 = (x1 * _rms_scale(x1)) * g_ref[...]
    nh = n2.astype(BF16)
    nl = (n2 - nh.astype(F32)).astype(BF16)
    n2_ref[...] = nh
    nt = ((1,), (1,))
    qt_ref[...] = _dot(wqh_ref[...], nh, nt) + (_dot(wqh_ref[...], nl, nt) + _dot(wql_ref[...], nh, nt))

    key_iota = lax.broadcasted_iota(jnp.int32, (1, PEER_NKEYS, LANES), 1).astype(F32)
    pos = jnp.broadcast_to(pos_ref[...], (pos_ref.shape[0], LANES))[None]

    def head_pair(hp, _):
        heads = [2 * hp, 2 * hp + 1]
        for h in heads:
            q = qt_ref[pl.ds(pl.multiple_of(h * PEER_DK, PEER_DK), PEER_DK), :]
            st_ref[h] = _dot3(kblk_ref[h], q).reshape(2, PEER_NKEYS, tm)
        for lb in range(tm // LANES):
            lanes = slice(lb * LANES, (lb + 1) * LANES)
            s = jnp.concatenate([st_ref[h, :, :, lanes] for h in heads], axis=0)
            marked, vals = _mark_top16(s, key_iota)
            for k, h in enumerate(heads):
                v_ref[h, :, :, lanes] = vals[2 * k:2 * k + 2]
                mark1_ref[h, :, lanes] = marked[2 * k]
                rankb_ref[h, :, lanes] = _rank_plus_half(marked[2 * k + 1]).astype(BF16)
                pb_ref[h, :, lanes] = jnp.exp(s[2 * k + 1] - vals[2 * k + 1, 0:1]).astype(BF16)
        return 0

    lax.fori_loop(0, PEER_HEADS // 2, head_pair, 0)

    for lb in range(tm // LANES):
        lanes = slice(lb * LANES, (lb + 1) * LANES)
        v1 = v_ref[:, 0, :, lanes]
        v2 = v_ref[:, 1, :, lanes]
        pieces = []
        for kind, fixed, start, count, first_valid in _candidate_slabs():
            if kind == "row":
                piece = v1[:, fixed:fixed + 1] + v2[:, start:start + count]
            else:
                piece = v1[:, start:start + count] + v2[:, fixed:fixed + 1]
            if first_valid:
                local = lax.broadcasted_iota(jnp.int32, piece.shape, 1)
                piece = jnp.where(local >= first_valid, piece, NEG_INF)
            pieces.append(piece)
        cand = jnp.concatenate(pieces, axis=1)
        picked = _is_marked(_mark_top16(cand, pos)[0])
        t0 = v1[:, 0:1] + v2[:, 0:1]
        z = jnp.sum(jnp.where(picked, jnp.exp(cand - t0), 0.0), axis=1, keepdims=True)
        count_i = [None] * PEER_TOPK
        off = 0
        for kind, fixed, start, count, first_valid in _candidate_slabs():
            hit = jnp.where(picked[:, off:off + count], 1.0, 0.0)
            if kind == "row":
                add = [(fixed, jnp.sum(hit, axis=1, keepdims=True))]
            else:
                add = [(start + r, hit[:, r:r + 1]) for r in range(first_valid, count)]
            for i, term in add:
                count_i[i] = term if count_i[i] is None else count_i[i] + term
            off += count
        for h in range(PEER_HEADS):
            marked1 = mark1_ref[h, :, lanes]
            len_a = jnp.zeros((PEER_NKEYS, LANES), F32)
            for i in range(PEER_TOPK):
                len_a = jnp.where(marked1 == MARK_BASE - i * MARK_STEP, count_i[i][h], len_a)
            lena_ref[h, :, lanes] = len_a
            ca_ref[h, :, lanes] = jnp.exp(st_ref[h, 0, :, lanes] - v1[h, 0:1]) / z[h]


def _retrieve(x1, g2, wq_hi, wq_lo, kblk, tm):
    n_tok = x1.shape[0]
    assert tm % LANES == 0, "selection runs on whole lane groups of tokens"
    pos = jnp.asarray(_candidate_positions())
    key_spec = pl.BlockSpec((PEER_HEADS, PEER_NKEYS, tm), lambda i: (0, 0, i))
    per_key = lambda dtype: jax.ShapeDtypeStruct((PEER_HEADS, PEER_NKEYS, n_tok), dtype)
    return pl.pallas_call(
        functools.partial(_retrieve_kernel, tm=tm),
        grid=(n_tok // tm,),
        in_specs=[
            pl.BlockSpec((tm, D_MODEL), lambda i: (i, 0)),
            pl.BlockSpec((1, D_MODEL), lambda i: (0, 0)),
            pl.BlockSpec((D_MODEL, D_MODEL), lambda i: (0, 0)),
            pl.BlockSpec((D_MODEL, D_MODEL), lambda i: (0, 0)),
            pl.BlockSpec((PEER_HEADS, 2 * PEER_NKEYS, PEER_DK), lambda i: (0, 0, 0)),
            pl.BlockSpec(pos.shape, lambda i: (0, 0)),
        ],
        out_specs=[pl.BlockSpec((tm, D_MODEL), lambda i: (i, 0)), key_spec, key_spec, key_spec, key_spec],
        out_shape=[jax.ShapeDtypeStruct((n_tok, D_MODEL), BF16), per_key(F32), per_key(F32),
                   per_key(BF16), per_key(BF16)],
        scratch_shapes=[
            pltpu.VMEM((D_MODEL, tm), F32),
            pltpu.VMEM((PEER_HEADS, 2, PEER_NKEYS, tm), F32),
            pltpu.VMEM((PEER_HEADS, 2, PEER_TOPK, tm), F32),
            pltpu.VMEM((PEER_HEADS, PEER_NKEYS, tm), F32),
        ],
        compiler_params=_params("arbitrary"),
        name="retrieve",
    )(x1, g2, wq_hi, wq_lo, kblk, pos)


EXPERT_TILE = 2048
A_PER_TILE = EXPERT_TILE // PEER_NKEYS


def _experts_kernel(x1_ref, n2_ref, u_ref, vt_ref, lena_ref, ca_ref, rankb_ref, pb_ref, fg_ref,
                    y_ref, acc_ref, *, tm):
    j = pl.program_id(1)

    @pl.when(j == 0)
    def _():
        acc_ref[...] = jnp.zeros_like(acc_ref)

    s = _dot(u_ref[...], n2_ref[...], ((1,), (1,)))
    pack = 2 * SUBLANES
    groups = PEER_NKEYS // pack
    acts = []
    for al in range(A_PER_TILE):
        w = None
        for h in range(PEER_HEADS):
            len_a = jnp.broadcast_to(lena_ref[h, al:al + 1, :], (pack, tm)).astype(BF16)
            c_a = jnp.broadcast_to(ca_ref[h, al:al + 1, :], (pack, tm)).astype(BF16)
            rank_b = rankb_ref[h].reshape(groups, pack, tm)
            p_b = pb_ref[h].reshape(groups, pack, tm)
            t = jnp.where(rank_b < len_a[None], p_b, jnp.zeros_like(p_b)) * c_a[None]
            w = t if w is None else w + t
        sa = s[al * PEER_NKEYS:(al + 1) * PEER_NKEYS]
        act = 0.5 * sa * (1.0 + lax.erf(sa * (1.0 / math.sqrt(2.0))))
        acts.append(act.astype(BF16) * w.reshape(PEER_NKEYS, tm))
    acc_ref[...] += _dot(vt_ref[...], jnp.concatenate(acts, axis=0))

    @pl.when(j == pl.num_programs(1) - 1)
    def _():
        x2 = x1_ref[...] + acc_ref[...].T
        y_ref[...] = (x2 * _rms_scale(x2)) * fg_ref[...]


def _experts(x1, n2, u_bf16, vt_bf16, len_a, c_a, rank_b, p_b, final_g, tm):
    n_tok = x1.shape[0]
    n_exp = u_bf16.shape[0]
    a_spec = pl.BlockSpec((PEER_HEADS, A_PER_TILE, tm), lambda i, j: (0, j, i))
    b_spec = pl.BlockSpec((PEER_HEADS, PEER_NKEYS, tm), lambda i, j: (0, 0, i))
    return pl.pallas_call(
        functools.partial(_experts_kernel, tm=tm),
        grid=(n_tok // tm, n_exp // EXPERT_TILE),
        in_specs=[
            pl.BlockSpec((tm, D_MODEL), lambda i, j: (i, 0)),
            pl.BlockSpec((tm, D_MODEL), lambda i, j: (i, 0)),
            pl.BlockSpec((EXPERT_TILE, D_MODEL), lambda i, j: (j, 0)),
            pl.BlockSpec((D_MODEL, EXPERT_TILE), lambda i, j: (0, j)),
            a_spec, a_spec, b_spec, b_spec,
            pl.BlockSpec((1, D_MODEL), lambda i, j: (0, 0)),
        ],
        out_specs=pl.BlockSpec((tm, D_MODEL), lambda i, j: (i, 0)),
        out_shape=jax.ShapeDtypeStruct((n_tok, D_MODEL), F32),
        scratch_shapes=[pltpu.VMEM((D_MODEL, tm), F32)],
        compiler_params=_params("arbitrary", "arbitrary"),
        name="experts",
    )(x1, n2, u_bf16, vt_bf16, len_a, c_a, rank_b, p_b, final_g)


def _token_tile(n_tok, cap):
    tm = cap
    while n_tok % tm:
        tm //= 2
    return tm


def _stream(x, s0, dn_buf0, h0, lru_buf0, p):
    b, t, _ = x.shape
    n_tok = b * t
    rows = min(CHUNK, t)
    xf = x.reshape(n_tok, D_MODEL)
    proj = _in_proj(xf, p["ln1_g"], p["w_in"], _token_tile(n_tok, 1024))
    proj3 = proj.reshape(b, t, PROJ_COLS)
    o_a, s_new, dn_buf = _deltanet(proj3, s0, dn_buf0, p["dn_conv_w"], p["alog_row"], p["dtb_row"],
                                   p["dn_norm_g"], rows, min(DN_CHUNKS_PER_STEP, t // rows))
    lru_rows = _token_tile(t, 256)
    o_b, h_new, lru_buf = _rglru(proj3, h0.reshape(b, 1, LRU_WIDTH), lru_buf0, p["lru_conv_w"],
                                 p["lru_conv_b"], p["lru_w_r"], p["lru_b_r"], p["lru_w_i"],
                                 p["lru_b_i"], p["lru_lambda"], lru_rows)
    x1 = _merge(xf, o_a.reshape(n_tok, DN_V), o_b.reshape(n_tok, LRU_WIDTH), proj,
                p["w_branch_a"], p["w_branch_b"], p["w_out"], _token_tile(n_tok, 512))
    tr = _token_tile(n_tok, 256)
    n2, len_a, c_a, rank_b, p_b = _retrieve(x1, p["ln2_g"], p["wq_hi"], p["wq_lo"], p["kblk"], tr)
    y = _experts(x1, n2, p["peer_u"], p["peer_vt"], len_a, c_a, rank_b, p_b, p["final_g"],
                 _token_tile(n_tok, 512))
    return (y.reshape(b, t, D_MODEL), s_new, dn_buf, h_new.reshape(b, LRU_WIDTH), lru_buf)


def _prepare(w_in, dn_conv_w, dn_a_log, dn_dt_bias, dn_norm_g, lru_conv_w, lru_conv_b, lru_w_r,
             lru_b_r, lru_w_i, lru_b_i, lru_lambda, w_branch_a, w_branch_b, w_out, ln1_g, ln2_g,
             peer_w_q, peer_keys, peer_u, peer_v, final_g):
    row = lambda v: v.reshape(1, -1).astype(F32)
    lane_row = lambda v: jnp.zeros((1, LANES), F32).at[0, DN_HEADS:2 * DN_HEADS].set(v)
    w_in_p = jnp.pad(w_in, ((0, 0), (0, PROJ_COLS - w_in.shape[1]))).astype(BF16)
    wq_t = peer_w_q.T
    wq_hi = wq_t.astype(BF16)
    wq_lo = (wq_t - wq_hi.astype(F32)).astype(BF16)
    zeros = jnp.zeros((PEER_HEADS, PEER_NKEYS, PEER_DKH), F32)
    kblk = jnp.concatenate([
        jnp.concatenate([peer_keys[:, 0], zeros], axis=-1),
        jnp.concatenate([zeros, peer_keys[:, 1]], axis=-1)], axis=1)
    return dict(
        w_in=w_in_p, dn_conv_w=dn_conv_w, alog_row=lane_row(dn_a_log), dtb_row=lane_row(dn_dt_bias),
        dn_norm_g=row(dn_norm_g), lru_conv_w=lru_conv_w, lru_conv_b=row(lru_conv_b),
        lru_w_r=lru_w_r.astype(BF16), lru_b_r=row(lru_b_r), lru_w_i=lru_w_i.astype(BF16),
        lru_b_i=row(lru_b_i), lru_lambda=row(lru_lambda), w_branch_a=w_branch_a.astype(BF16),
        w_branch_b=w_branch_b.astype(BF16), w_out=w_out.astype(BF16), ln1_g=row(ln1_g),
        ln2_g=row(ln2_g), wq_hi=wq_hi, wq_lo=wq_lo, kblk=kblk, peer_u=peer_u.astype(BF16),
        peer_vt=peer_v.T.astype(BF16), final_g=row(final_g))


def kernel(x_prompt, x_sample, state_dn, state_dn_conv, state_lru_h, state_lru_conv, w_in, dn_conv_w, dn_a_log, dn_dt_bias, dn_norm_g, lru_conv_w, lru_conv_b, lru_w_r, lru_b_r, lru_w_i, lru_b_i, lru_lambda, w_branch_a, w_branch_b, w_out, ln1_g, ln2_g, peer_w_q, peer_keys, peer_u, peer_v, final_g):
    depth = w_in.shape[0]
    assert depth == 1, "the final norm is fused into the layer's last kernel"
    bp = x_prompt.shape[0]
    dt = x_prompt.dtype
    p = _prepare(w_in[0], dn_conv_w[0], dn_a_log[0], dn_dt_bias[0], dn_norm_g[0], lru_conv_w[0],
                 lru_conv_b[0], lru_w_r[0], lru_b_r[0], lru_w_i[0], lru_b_i[0], lru_lambda[0],
                 w_branch_a[0], w_branch_b[0], w_out[0], ln1_g[0], ln2_g[0], peer_w_q[0],
                 peer_keys[0], peer_u[0], peer_v[0], final_g)
    prompt = _stream(
        x_prompt,
        jnp.zeros((bp, DN_HEADS, DN_DK, DN_DV), dt),
        jnp.zeros((bp, CONV_W - 1, DN_CONV_CH), dt),
        jnp.zeros((bp, LRU_WIDTH), dt),
        jnp.zeros((bp, CONV_W - 1, LRU_WIDTH), dt),
        p)
    sample = _stream(x_sample, state_dn[0], state_dn_conv[0], state_lru_h[0], state_lru_conv[0], p)
    y_p, s_p, db_p, h_p, lb_p = prompt
    y_s, s_s, db_s, h_s, lb_s = sample
    return (y_p, y_s, s_p[None], db_p[None], h_p[None], lb_p[None],
            s_s[None], db_s[None], h_s[None], lb_s[None])
```

```python
import functools
import math

import jax
import jax.numpy as jnp
import numpy as np
from jax import lax
from jax.experimental import pallas as pl
from jax.experimental.pallas import tpu as pltpu

F32 = jnp.float32
BF16 = jnp.bfloat16

D_MODEL = 1024
CONV_W = 4
EPS = 1e-6
DN_HEADS = 8
DN_DK = 128
DN_DV = 128
DN_QK = DN_HEADS * DN_DK
DN_V = DN_HEADS * DN_DV
DN_CONV_CH = 2 * DN_QK + DN_V
LRU_WIDTH = D_MODEL
LRU_BLOCKS = 8
LRU_BDIM = LRU_WIDTH // LRU_BLOCKS
LRU_C = 8.0
PEER_HEADS = 8
PEER_NKEYS = 128
PEER_DK = 128
PEER_DKH = PEER_DK // 2
PEER_TOPK = 16
CHUNK = 64

LANES = 128
SUBLANES = 8
MAIN_COLS = 8192
PROJ_COLS = MAIN_COLS + LANES
PROJ_COL_TILE = PROJ_COLS // 5
BD_BLOCK = MAIN_COLS // LANES
TAIL_ROW = SUBLANES - (CONV_W - 1)
VMEM_LIMIT = 56 * 1024 * 1024
NEG_INF = float("-inf")


def _params(*sem):
    return pltpu.CompilerParams(dimension_semantics=sem, vmem_limit_bytes=VMEM_LIMIT)


def _split3(x):
    a = x.astype(BF16)
    r = x - a.astype(F32)
    b = r.astype(BF16)
    c = (r - b.astype(F32)).astype(BF16)
    return a, b, c


def _dot(a, b, dims=None):
    if dims is None:
        return jnp.dot(a, b, preferred_element_type=F32)
    return lax.dot_general(a, b, (dims, ((), ())), preferred_element_type=F32)


def _dot3(a, b, dims=None):
    ah = a.astype(BF16)
    al = (a - ah.astype(F32)).astype(BF16)
    bh = b.astype(BF16)
    bl = (b - bh.astype(F32)).astype(BF16)
    return _dot(ah, bh, dims) + (_dot(ah, bl, dims) + _dot(al, bh, dims))


def _dot_exact_lhs(a_bf16, b, dims=None):
    b0, b1, b2 = _split3(b)
    return _dot(a_bf16, b0, dims) + (_dot(a_bf16, b1, dims) + _dot(a_bf16, b2, dims))


def _sigmoid(x):
    return 1.0 / (1.0 + jnp.exp(-x))


def _softplus(x):
    return jnp.maximum(x, 0.0) + jnp.log1p(jnp.exp(-jnp.abs(x)))


def _rms_scale(x):
    return lax.rsqrt(jnp.mean(x * x, axis=-1, keepdims=True) + EPS)


def _causal_conv(xpad_ref, new_rows, w, rows):
    xpad_ref[SUBLANES:SUBLANES + rows, :] = new_rows
    y = xpad_ref[TAIL_ROW:TAIL_ROW + rows, :] * w[0:1, :]
    for j in range(1, CONV_W):
        y = y + xpad_ref[TAIL_ROW + j:TAIL_ROW + j + rows, :] * w[j:j + 1, :]
    tail = xpad_ref[rows + TAIL_ROW:rows + SUBLANES, :]
    xpad_ref[TAIL_ROW:SUBLANES, :] = tail
    return y, tail


def _inproj_kernel(x_ref, g_ref, w_ref, o_ref):
    x = x_ref[...]
    n = (x * _rms_scale(x)) * g_ref[...]
    o_ref[...] = _dot(n.astype(BF16), w_ref[...])


def _in_proj(x, g, w_bf16, tm):
    n_tok = x.shape[0]
    return pl.pallas_call(
        _inproj_kernel,
        grid=(PROJ_COLS // PROJ_COL_TILE, n_tok // tm),
        in_specs=[
            pl.BlockSpec((tm, D_MODEL), lambda j, i: (i, 0)),
            pl.BlockSpec((1, D_MODEL), lambda j, i: (0, 0)),
            pl.BlockSpec((D_MODEL, PROJ_COL_TILE), lambda j, i: (0, j)),
        ],
        out_specs=pl.BlockSpec((tm, PROJ_COL_TILE), lambda j, i: (i, j)),
        out_shape=jax.ShapeDtypeStruct((n_tok, PROJ_COLS), F32),
        compiler_params=_params("arbitrary", "arbitrary"),
        name="in_proj",
    )(x, g, w_bf16)


def _bdot(a, b, eq):
    return jnp.einsum(eq, a, b, preferred_element_type=F32)


def _bdot1(a, b, eq):
    return _bdot(a.astype(BF16), b.astype(BF16), eq)


def _bdot3(a, b, eq):
    ah = a.astype(BF16)
    al = (a - ah.astype(F32)).astype(BF16)
    bh = b.astype(BF16)
    bl = (b - bh.astype(F32)).astype(BF16)
    return _bdot(ah, bh, eq) + (_bdot(ah, bl, eq) + _bdot(al, bh, eq))


MM = "gij,gjk->gik"
DN_CHUNKS_PER_STEP = 4


def _unit_lower_inverse(nmat, rows, row_i, col_i):
    eye = jnp.where(row_i == col_i, 1.0, 0.0)
    blk = SUBLANES
    differ = row_i ^ col_i
    nd = jnp.where(differ < blk, nmat, 0.0)
    t = eye - nd
    m = _bdot3(nd, nd, MM)
    t = t + _bdot3(t, m, MM)
    m = _bdot3(m, m, MM)
    t = t + _bdot3(t, m, MM)
    while blk < rows:
        off = jnp.where((differ >> int(math.log2(blk))) == 1, nmat, 0.0)
        t = t - _bdot3(t, _bdot3(off, t, MM), MM)
        blk *= 2
    return t


def _deltanet_kernel(qkv_ref, z_ref, bd_ref, s0_ref, buf0_ref, convw_ref, alog_ref, dtb_ref,
                     ng_ref, o_ref, s_ref, buf_ref, xpad_ref, *, rows, chunks):
    span = rows * chunks

    @pl.when(pl.program_id(1) == 0)
    def _():
        xpad_ref[TAIL_ROW:SUBLANES, :] = buf0_ref[0]
        s_ref[...] = s0_ref[...]

    y, tail = _causal_conv(xpad_ref, qkv_ref[0], convw_ref[...], span)
    buf_ref[0] = tail
    act = y * _sigmoid(y)

    bd = bd_ref[0]
    beta_all = _sigmoid(bd)
    g_all = -jnp.exp(alog_ref[...]) * _softplus(bd + dtb_ref[...])
    t_row = lax.broadcasted_iota(jnp.int32, (span, span), 0)
    t_col = lax.broadcasted_iota(jnp.int32, (span, span), 1)
    tri = jnp.where((t_row ^ t_col) < rows, jnp.where(t_row >= t_col, 1.0, 0.0), 0.0).astype(BF16)
    gcum = _dot_exact_lhs(tri, g_all)
    pad = -span % LANES
    gcum_t = (jnp.concatenate([gcum, jnp.zeros((pad, LANES), F32)], axis=0) if pad else gcum).T

    def per_head(src, col0, width):
        return jnp.stack([src[c * rows:(c + 1) * rows, col0 + h * width:col0 + (h + 1) * width]
                          for c in range(chunks) for h in range(DN_HEADS)])

    q = per_head(act, 0, DN_DK)
    k = per_head(act, DN_QK, DN_DK)
    v = per_head(act, 2 * DN_QK, DN_DV)
    z = per_head(z_ref[0], 0, DN_DV)
    beta = per_head(beta_all, 0, 1)
    gc = per_head(gcum, DN_HEADS, 1)
    gr = jnp.stack([gcum_t[DN_HEADS + h:DN_HEADS + h + 1, c * rows:(c + 1) * rows]
                    for c in range(chunks) for h in range(DN_HEADS)])
    g_last = gc[:, rows - 1:rows, :]

    q = q * (lax.rsqrt(jnp.sum(q * q, axis=-1, keepdims=True) + EPS) * (DN_DK ** -0.5))
    k = k * lax.rsqrt(jnp.sum(k * k, axis=-1, keepdims=True) + EPS)
    row_i = lax.broadcasted_iota(jnp.int32, (rows, rows), 0)
    col_i = lax.broadcasted_iota(jnp.int32, (rows, rows), 1)
    incl = row_i >= col_i
    dec = jnp.where(incl, jnp.exp(jnp.where(incl, gc - gr, 0.0)), 0.0)
    kb = k.astype(BF16)
    kk = _bdot(kb, kb, "gik,gjk->gij")
    nmat = jnp.where(row_i > col_i, beta * kk * dec, 0.0)
    gam = jnp.exp(gc)
    rhs = jnp.concatenate([(beta * gam) * k, beta * v], axis=-1)
    sol = _bdot1(_unit_lower_inverse(nmat, rows, row_i, col_i), rhs, MM)
    uv = sol[:, :, DN_DK:]
    wq = jnp.concatenate([sol[:, :, :DN_DK], q * gam], axis=1)
    qk = _bdot(q.astype(BF16), kb, "gik,gjk->gij") * dec
    kend = k * jnp.exp(g_last - gc)
    gend = jnp.exp(g_last)
    gate = ng_ref[...] * (z * _sigmoid(z))

    for c in range(chunks):
        hs = slice(c * DN_HEADS, (c + 1) * DN_HEADS)
        s_old = s_ref[0]
        ws = _bdot1(wq[hs], s_old, MM)
        u = uv[hs] - ws[:, :rows]
        ub = u.astype(BF16)
        o = ws[:, rows:] + _bdot(qk[hs].astype(BF16), ub, MM)
        s_ref[0] = gend[hs] * s_old + _bdot(kend[hs].astype(BF16), ub, "glk,glv->gkv")
        o = (o * _rms_scale(o)) * gate[hs]
        for h in range(DN_HEADS):
            o_ref[0, c * rows:(c + 1) * rows, h * DN_DV:(h + 1) * DN_DV] = o[h]


def _deltanet(proj3, s0, buf0, convw, alog_row, dtb_row, norm_g, rows, chunks):
    b, t, _ = proj3.shape
    span = rows * chunks
    nc = t // span
    rows, rows_per_chunk = span, rows
    return pl.pallas_call(
        functools.partial(_deltanet_kernel, rows=rows_per_chunk, chunks=chunks),
        grid=(b, nc),
        in_specs=[
            pl.BlockSpec((1, rows, DN_CONV_CH), lambda i, c: (i, c, 0)),
            pl.BlockSpec((1, rows, DN_V), lambda i, c: (i, c, DN_CONV_CH // DN_V)),
            pl.BlockSpec((1, rows, LANES), lambda i, c: (i, c, BD_BLOCK)),
            pl.BlockSpec((1, DN_HEADS, DN_DK, DN_DV), lambda i, c: (i, 0, 0, 0)),
            pl.BlockSpec((1, CONV_W - 1, DN_CONV_CH), lambda i, c: (i, 0, 0)),
            pl.BlockSpec((CONV_W, DN_CONV_CH), lambda i, c: (0, 0)),
            pl.BlockSpec((1, LANES), lambda i, c: (0, 0)),
            pl.BlockSpec((1, LANES), lambda i, c: (0, 0)),
            pl.BlockSpec((1, DN_DV), lambda i, c: (0, 0)),
        ],
        out_specs=[
            pl.BlockSpec((1, rows, DN_V), lambda i, c: (i, c, 0)),
            pl.BlockSpec((1, DN_HEADS, DN_DK, DN_DV), lambda i, c: (i, 0, 0, 0)),
            pl.BlockSpec((1, CONV_W - 1, DN_CONV_CH), lambda i, c: (i, 0, 0)),
        ],
        out_shape=[
            jax.ShapeDtypeStruct((b, t, DN_V), F32),
            jax.ShapeDtypeStruct((b, DN_HEADS, DN_DK, DN_DV), F32),
            jax.ShapeDtypeStruct((b, CONV_W - 1, DN_CONV_CH), F32),
        ],
        scratch_shapes=[pltpu.VMEM((SUBLANES + rows, DN_CONV_CH), F32)],
        compiler_params=_params("arbitrary", "arbitrary"),
        name="deltanet",
    )(proj3, proj3, proj3, s0, buf0, convw, alog_row, dtb_row, norm_g)


def _rglru_kernel(lx_ref, ly_ref, h0_ref, buf0_ref, convw_ref, convb_ref, wr_ref, br_ref,
                  wi_ref, bi_ref, lam_ref, o_ref, h_ref, buf_ref, xpad_ref, a_ref, b_ref, *, rows):
    c = pl.program_id(1)

    @pl.when(c == 0)
    def _():
        xpad_ref[TAIL_ROW:SUBLANES, :] = buf0_ref[0]
        h_ref[...] = h0_ref[...]

    y, tail = _causal_conv(xpad_ref, lx_ref[0], convw_ref[...], rows)
    buf_ref[0] = tail
    x = y + convb_ref[...]
    xb = x.astype(BF16)
    sp = LRU_C * _softplus(-lam_ref[...])
    for n in range(LRU_BLOCKS):
        lo = n * LRU_BDIM
        xn = xb[:, lo:lo + LRU_BDIM]
        r = _sigmoid(_dot(xn, wr_ref[n]) + br_ref[:, lo:lo + LRU_BDIM])
        i = _sigmoid(_dot(xn, wi_ref[n]) + bi_ref[:, lo:lo + LRU_BDIM])
        log_a = -(r * sp[:, lo:lo + LRU_BDIM])
        a = jnp.exp(log_a)
        a_ref[:, lo:lo + LRU_BDIM] = a
        one_minus_a2 = -jnp.tanh(log_a) * (a * a + 1.0)
        b_ref[:, lo:lo + LRU_BDIM] = jnp.sqrt(one_minus_a2) * (i * x[:, lo:lo + LRU_BDIM])

    groups = rows // SUBLANES
    a = a_ref[...].reshape(groups, SUBLANES, LRU_WIDTH)
    b = b_ref[...].reshape(groups, SUBLANES, LRU_WIDTH)
    sub = lax.broadcasted_iota(jnp.int32, (groups, SUBLANES, LRU_WIDTH), 1)
    d = 1
    while d < SUBLANES:
        keep = sub >= d
        a_prev = jnp.where(keep, pltpu.roll(a, d, 1), 1.0)
        b_prev = jnp.where(keep, pltpu.roll(b, d, 1), 0.0)
        b = a * b_prev + b
        a = a * a_prev
        d *= 2
    a_ref[...] = a.reshape(rows, LRU_WIDTH)
    b_ref[...] = b.reshape(rows, LRU_WIDTH)

    def group_step(g, h):
        r0 = pl.multiple_of(g * SUBLANES, SUBLANES)
        hs = a_ref[pl.ds(r0, SUBLANES), :] * h + b_ref[pl.ds(r0, SUBLANES), :]
        o_ref[0, pl.ds(r0, SUBLANES), :] = hs
        return hs[SUBLANES - 1:SUBLANES, :]

    h_ref[0] = lax.fori_loop(0, groups, group_step, h_ref[0], unroll=4)
    o_ref[0] = o_ref[0] * jax.nn.gelu(ly_ref[0], approximate=True)


def _rglru(proj3, h0, buf0, convw, convb, w_r, b_r, w_i, b_i, lam, rows):
    b, t, _ = proj3.shape
    nc = t // rows
    row = lambda i, c: (0, 0)
    return pl.pallas_call(
        functools.partial(_rglru_kernel, rows=rows),
        grid=(b, nc),
        in_specs=[
            pl.BlockSpec((1, rows, LRU_WIDTH), lambda i, c: (i, c, 4)),
            pl.BlockSpec((1, rows, LRU_WIDTH), lambda i, c: (i, c, 5)),
            pl.BlockSpec((1, 1, LRU_WIDTH), lambda i, c: (i, 0, 0)),
            pl.BlockSpec((1, CONV_W - 1, LRU_WIDTH), lambda i, c: (i, 0, 0)),
            pl.BlockSpec((CONV_W, LRU_WIDTH), row),
            pl.BlockSpec((1, LRU_WIDTH), row),
            pl.BlockSpec((LRU_BLOCKS, LRU_BDIM, LRU_BDIM), lambda i, c: (0, 0, 0)),
            pl.BlockSpec((1, LRU_WIDTH), row),
            pl.BlockSpec((LRU_BLOCKS, LRU_BDIM, LRU_BDIM), lambda i, c: (0, 0, 0)),
            pl.BlockSpec((1, LRU_WIDTH), row),
            pl.BlockSpec((1, LRU_WIDTH), row),
        ],
        out_specs=[
            pl.BlockSpec((1, rows, LRU_WIDTH), lambda i, c: (i, c, 0)),
            pl.BlockSpec((1, 1, LRU_WIDTH), lambda i, c: (i, 0, 0)),
            pl.BlockSpec((1, CONV_W - 1, LRU_WIDTH), lambda i, c: (i, 0, 0)),
        ],
        out_shape=[
            jax.ShapeDtypeStruct((b, t, LRU_WIDTH), F32),
            jax.ShapeDtypeStruct((b, 1, LRU_WIDTH), F32),
            jax.ShapeDtypeStruct((b, CONV_W - 1, LRU_WIDTH), F32),
        ],
        scratch_shapes=[
            pltpu.VMEM((SUBLANES + rows, LRU_WIDTH), F32),
            pltpu.VMEM((rows, LRU_WIDTH), F32),
            pltpu.VMEM((rows, LRU_WIDTH), F32),
        ],
        compiler_params=_params("arbitrary", "arbitrary"),
        name="rglru",
    )(proj3, proj3, h0, buf0, convw, convb, w_r, b_r, w_i, b_i, lam)


def _merge_kernel(x_ref, oa_ref, ob_ref, ga_ref, gb_ref, wa_ref, wb_ref, wo_ref, x1_ref):
    ya = _dot(oa_ref[...].astype(BF16), wa_ref[...])
    yb = _dot(ob_ref[...].astype(BF16), wb_ref[...])
    mix = _sigmoid(ga_ref[...]) * ya + _sigmoid(gb_ref[...]) * yb
    x1_ref[...] = x_ref[...] + _dot(mix.astype(BF16), wo_ref[...])


def _merge(x, o_a, o_b, proj, w_a, w_b, w_o, tm):
    n_tok = x.shape[0]
    tok = lambda i: (i, 0)
    full = lambda i: (0, 0)
    return pl.pallas_call(
        _merge_kernel,
        grid=(n_tok // tm,),
        in_specs=[
            pl.BlockSpec((tm, D_MODEL), tok),
            pl.BlockSpec((tm, DN_V), tok),
            pl.BlockSpec((tm, LRU_WIDTH), tok),
            pl.BlockSpec((tm, D_MODEL), lambda i: (i, 6)),
            pl.BlockSpec((tm, D_MODEL), lambda i: (i, 7)),
            pl.BlockSpec((DN_V, D_MODEL), full),
            pl.BlockSpec((LRU_WIDTH, D_MODEL), full),
            pl.BlockSpec((D_MODEL, D_MODEL), full),
        ],
        out_specs=pl.BlockSpec((tm, D_MODEL), tok),
        out_shape=jax.ShapeDtypeStruct((n_tok, D_MODEL), F32),
        compiler_params=_params("arbitrary"),
        name="merge",
    )(x, o_a, o_b, proj, proj, w_a, w_b, w_o)


def _candidate_slabs():
    return (
        ("row", 0, 0, 16, 0), ("row", 1, 0, 8, 0), ("row", 2, 0, 8, 0), ("row", 3, 0, 8, 0),
        ("col", 0, 8, 8, 0), ("col", 0, 0, 8, 4), ("col", 1, 0, 8, 4), ("col", 2, 0, 8, 4),
    )


def _candidate_positions():
    pos = []
    for kind, fixed, start, count, first_valid in _candidate_slabs():
        for r in range(start, start + count):
            i, j = (fixed, r) if kind == "row" else (r, fixed)
            masked = r - start < first_valid
            pos.append(PEER_TOPK * PEER_TOPK + len(pos) if masked else i * PEER_TOPK + j)
    assert len(set(pos)) == len(pos)
    return np.asarray(pos, np.float32).reshape(-1, 1)


MARK_BASE = -(2.0 ** 127 + 2.0 ** 126)
MARK_STEP = 2.0 ** 120
ORDER_BIG = float(1 << 20)


def _mark_top16(s, order):
    vals = []
    for r in range(PEER_TOPK):
        m = jnp.max(s, axis=1, keepdims=True)
        first = jnp.min(jnp.where(s == m, order, ORDER_BIG), axis=1, keepdims=True)
        s = jnp.where(order == first, MARK_BASE - r * MARK_STEP, s)
        vals.append(m)
    return s, jnp.concatenate(vals, axis=1)


def _is_marked(x):
    return (x <= MARK_BASE) & (x > MARK_BASE - PEER_TOPK * MARK_STEP)


def _rank_plus_half(marked):
    return jnp.where(_is_marked(marked), (MARK_BASE - marked) * (1.0 / MARK_STEP), float(PEER_TOPK)) + 0.5


def _retrieve_kernel(x1_ref, g_ref, wqh_ref, wql_ref, kblk_ref, pos_ref,
                     n2_ref, lena_ref, ca_ref, rankb_ref, pb_ref,
                     qt_ref, st_ref, v_ref, mark1_ref, *, tm):
    x1 = x1_ref[...]
    n2 = (x1 * _rms_scale(x1)) * g_ref[...]
    nh = n2.astype(BF16)
    nl = (n2 - nh.astype(F32)).astype(BF16)
    n2_ref[...] = nh
    nt = ((1,), (1,))
    qt_ref[...] = _dot(wqh_ref[...], nh, nt) + (_dot(wqh_ref[...], nl, nt) + _dot(wql_ref[...], nh, nt))

    key_iota = lax.broadcasted_iota(jnp.int32, (1, PEER_NKEYS, LANES), 1).astype(F32)
    pos = jnp.broadcast_to(pos_ref[...], (pos_ref.shape[0], LANES))[None]

    def head_pair(hp, _):
        heads = [2 * hp, 2 * hp + 1]
        for h in heads:
            q = qt_ref[pl.ds(pl.multiple_of(h * PEER_DK, PEER_DK), PEER_DK), :]
            st_ref[h] = _dot3(kblk_ref[h], q).reshape(2, PEER_NKEYS, tm)
        for lb in range(tm // LANES):
            lanes = slice(lb * LANES, (lb + 1) * LANES)
            s = jnp.concatenate([st_ref[h, :, :, lanes] for h in heads], axis=0)
            marked, vals = _mark_top16(s, key_iota)
            for k, h in enumerate(heads):
                v_ref[h, :, :, lanes] = vals[2 * k:2 * k + 2]
                mark1_ref[h, :, lanes] = marked[2 * k]
                rankb_ref[h, :, lanes] = _rank_plus_half(marked[2 * k + 1]).astype(BF16)
                pb_ref[h, :, lanes] = jnp.exp(s[2 * k + 1] - vals[2 * k + 1, 0:1]).astype(BF16)
        return 0

    lax.fori_loop(0, PEER_HEADS // 2, head_pair, 0)

    for lb in range(tm // LANES):
        lanes = slice(lb * LANES, (lb + 1) * LANES)
        v1 = v_ref[:, 0, :, lanes]
        v2 = v_ref[:, 1, :, lanes]
        pieces = []
        for kind, fixed, start, count, first_valid in _candidate_slabs():
            if kind == "row":
                piece = v1[:, fixed:fixed + 1] + v2[:, start:start + count]
            else:
                piece = v1[:, start:start + count] + v2[:, fixed:fixed + 1]
            if first_valid:
                local = lax.broadcasted_iota(jnp.int32, piece.shape, 1)
                piece = jnp.where(local >= first_valid, piece, NEG_INF)
            pieces.append(piece)
        cand = jnp.concatenate(pieces, axis=1)
        picked = _is_marked(_mark_top16(cand, pos)[0])
        t0 = v1[:, 0:1] + v2[:, 0:1]
        z = jnp.sum(jnp.where(picked, jnp.exp(cand - t0), 0.0), axis=1, keepdims=True)
        count_i = [None] * PEER_TOPK
        off = 0
        for kind, fixed, start, count, first_valid in _candidate_slabs():
            hit = jnp.where(picked[:, off:off + count], 1.0, 0.0)
            if kind == "row":
                add = [(fixed, jnp.sum(hit, axis=1, keepdims=True))]
            else:
                add = [(start + r, hit[:, r:r + 1]) for r in range(first_valid, count)]
            for i, term in add:
                count_i[i] = term if count_i[i] is None else count_i[i] + term
            off += count
        for h in range(PEER_HEADS):
            marked1 = mark1_ref[h, :, lanes]
            len_a = jnp.zeros((PEER_NKEYS, LANES), F32)
            for i in range(PEER_TOPK):
                len_a = jnp.where(marked1 == MARK_BASE - i * MARK_STEP, count_i[i][h], len_a)
            lena_ref[h, :, lanes] = len_a
            ca_ref[h, :, lanes] = jnp.exp(st_ref[h, 0, :, lanes] - v1[h, 0:1]) / z[h]


def _retrieve(x1, g2, wq_hi, wq_lo, kblk, tm):
    n_tok = x1.shape[0]
    assert tm % LANES == 0, "selection runs on whole lane groups of tokens"
    pos = jnp.asarray(_candidate_positions())
    key_spec = pl.BlockSpec((PEER_HEADS, PEER_NKEYS, tm), lambda i: (0, 0, i))
    per_key = lambda dtype: jax.ShapeDtypeStruct((PEER_HEADS, PEER_NKEYS, n_tok), dtype)
    return pl.pallas_call(
        functools.partial(_retrieve_kernel, tm=tm),
        grid=(n_tok // tm,),
        in_specs=[
            pl.BlockSpec((tm, D_MODEL), lambda i: (i, 0)),
            pl.BlockSpec((1, D_MODEL), lambda i: (0, 0)),
            pl.BlockSpec((D_MODEL, D_MODEL), lambda i: (0, 0)),
            pl.BlockSpec((D_MODEL, D_MODEL), lambda i: (0, 0)),
            pl.BlockSpec((PEER_HEADS, 2 * PEER_NKEYS, PEER_DK), lambda i: (0, 0, 0)),
            pl.BlockSpec(pos.shape, lambda i: (0, 0)),
        ],
        out_specs=[pl.BlockSpec((tm, D_MODEL), lambda i: (i, 0)), key_spec, key_spec, key_spec, key_spec],
        out_shape=[jax.ShapeDtypeStruct((n_tok, D_MODEL), BF16), per_key(F32), per_key(F32),
                   per_key(BF16), per_key(BF16)],
        scratch_shapes=[
            pltpu.VMEM((D_MODEL, tm), F32),
            pltpu.VMEM((PEER_HEADS, 2, PEER_NKEYS, tm), F32),
            pltpu.VMEM((PEER_HEADS, 2, PEER_TOPK, tm), F32),
            pltpu.VMEM((PEER_HEADS, PEER_NKEYS, tm), F32),
        ],
        compiler_params=_params("arbitrary"),
        name="retrieve",
    )(x1, g2, wq_hi, wq_lo, kblk, pos)


EXPERT_TILE = 1024
A_PER_TILE = EXPERT_TILE // PEER_NKEYS


def _experts_kernel(x1_ref, n2_ref, u_ref, vt_ref, lena_ref, ca_ref, rankb_ref, pb_ref, fg_ref,
                    y_ref, acc_ref, *, tm):
    j = pl.program_id(1)

    @pl.when(j == 0)
    def _():
        acc_ref[...] = jnp.zeros_like(acc_ref)

    s = _dot(u_ref[...], n2_ref[...], ((1,), (1,)))
    pack = 2 * SUBLANES
    groups = PEER_NKEYS // pack
    acts = []
    for al in range(A_PER_TILE):
        w = None
        for h in range(PEER_HEADS):
            len_a = jnp.broadcast_to(lena_ref[h, al:al + 1, :], (pack, tm)).astype(BF16)
            c_a = jnp.broadcast_to(ca_ref[h, al:al + 1, :], (pack, tm)).astype(BF16)
            rank_b = rankb_ref[h].reshape(groups, pack, tm)
            p_b = pb_ref[h].reshape(groups, pack, tm)
            t = jnp.where(rank_b < len_a[None], p_b, jnp.zeros_like(p_b)) * c_a[None]
            w = t if w is None else w + t
        sa = s[al * PEER_NKEYS:(al + 1) * PEER_NKEYS]
        act = 0.5 * sa * (1.0 + lax.erf(sa * (1.0 / math.sqrt(2.0))))
        acts.append(act.astype(BF16) * w.reshape(PEER_NKEYS, tm))
    acc_ref[...] += _dot(vt_ref[...], jnp.concatenate(acts, axis=0))

    @pl.when(j == pl.num_programs(1) - 1)
    def _():
        x2 = x1_ref[...] + acc_ref[...].T
        y_ref[...] = (x2 * _rms_scale(x2)) * fg_ref[...]


def _experts(x1, n2, u_bf16, vt_bf16, len_a, c_a, rank_b, p_b, final_g, tm):
    n_tok = x1.shape[0]
    n_exp = u_bf16.shape[0]
    a_spec = pl.BlockSpec((PEER_HEADS, A_PER_TILE, tm), lambda i, j: (0, j, i))
    b_spec = pl.BlockSpec((PEER_HEADS, PEER_NKEYS, tm), lambda i, j: (0, 0, i))
    return pl.pallas_call(
        functools.partial(_experts_kernel, tm=tm),
        grid=(n_tok // tm, n_exp // EXPERT_TILE),
        in_specs=[
            pl.BlockSpec((tm, D_MODEL), lambda i, j: (i, 0)),
            pl.BlockSpec((tm, D_MODEL), lambda i, j: (i, 0)),
            pl.BlockSpec((EXPERT_TILE, D_MODEL), lambda i, j: (j, 0)),
            pl.BlockSpec((D_MODEL, EXPERT_TILE), lambda i, j: (0, j)),
            a_spec, a_spec, b_spec, b_spec,
            pl.BlockSpec((1, D_MODEL), lambda i, j: (0, 0)),
        ],
        out_specs=pl.BlockSpec((tm, D_MODEL), lambda i, j: (i, 0)),
        out_shape=jax.ShapeDtypeStruct((n_tok, D_MODEL), F32),
        scratch_shapes=[pltpu.VMEM((D_MODEL, tm), F32)],
        compiler_params=_params("arbitrary", "arbitrary"),
        name="experts",
    )(x1, n2, u_bf16, vt_bf16, len_a, c_a, rank_b, p_b, final_g)


def _token_tile(n_tok, cap):
    tm = cap
    while n_tok % tm:
        tm //= 2
    return tm


def _stream(x, s0, dn_buf0, h0, lru_buf0, p):
    b, t, _ = x.shape
    n_tok = b * t
    rows = min(CHUNK, t)
    xf = x.reshape(n_tok, D_MODEL)
    proj = _in_proj(xf, p["ln1_g"], p["w_in"], _token_tile(n_tok, 1024))
    proj3 = proj.reshape(b, t, PROJ_COLS)
    o_a, s_new, dn_buf = _deltanet(proj3, s0, dn_buf0, p["dn_conv_w"], p["alog_row"], p["dtb_row"],
                                   p["dn_norm_g"], rows, min(DN_CHUNKS_PER_STEP, t // rows))
    lru_rows = _token_tile(t, 256)
    o_b, h_new, lru_buf = _rglru(proj3, h0.reshape(b, 1, LRU_WIDTH), lru_buf0, p["lru_conv_w"],
                                 p["lru_conv_b"], p["lru_w_r"], p["lru_b_r"], p["lru_w_i"],
                                 p["lru_b_i"], p["lru_lambda"], lru_rows)
    x1 = _merge(xf, o_a.reshape(n_tok, DN_V), o_b.reshape(n_tok, LRU_WIDTH), proj,
                p["w_branch_a"], p["w_branch_b"], p["w_out"], _token_tile(n_tok, 512))
    tr = _token_tile(n_tok, 256)
    n2, len_a, c_a, rank_b, p_b = _retrieve(x1, p["ln2_g"], p["wq_hi"], p["wq_lo"], p["kblk"], tr)
    y = _experts(x1, n2, p["peer_u"], p["peer_vt"], len_a, c_a, rank_b, p_b, p["final_g"],
                 _token_tile(n_tok, 1024))
    return (y.reshape(b, t, D_MODEL), s_new, dn_buf, h_new.reshape(b, LRU_WIDTH), lru_buf)


def _prepare(w_in, dn_conv_w, dn_a_log, dn_dt_bias, dn_norm_g, lru_conv_w, lru_conv_b, lru_w_r,
             lru_b_r, lru_w_i, lru_b_i, lru_lambda, w_branch_a, w_branch_b, w_out, ln1_g, ln2_g,
             peer_w_q, peer_keys, peer_u, peer_v, final_g):
    row = lambda v: v.reshape(1, -1).astype(F32)
    lane_row = lambda v: jnp.zeros((1, LANES), F32).at[0, DN_HEADS:2 * DN_HEADS].set(v)
    w_in_p = jnp.pad(w_in, ((0, 0), (0, PROJ_COLS - w_in.shape[1]))).astype(BF16)
    wq_t = peer_w_q.T
    wq_hi = wq_t.astype(BF16)
    wq_lo = (wq_t - wq_hi.astype(F32)).astype(BF16)
    zeros = jnp.zeros((PEER_HEADS, PEER_NKEYS, PEER_DKH), F32)
    kblk = jnp.concatenate([
        jnp.concatenate([peer_keys[:, 0], zeros], axis=-1),
        jnp.concatenate([zeros, peer_keys[:, 1]], axis=-1)], axis=1)
    return dict(
        w_in=w_in_p, dn_conv_w=dn_conv_w, alog_row=lane_row(dn_a_log), dtb_row=lane_row(dn_dt_bias),
        dn_norm_g=row(dn_norm_g), lru_conv_w=lru_conv_w, lru_conv_b=row(lru_conv_b),
        lru_w_r=lru_w_r.astype(BF16), lru_b_r=row(lru_b_r), lru_w_i=lru_w_i.astype(BF16),
        lru_b_i=row(lru_b_i), lru_lambda=row(lru_lambda), w_branch_a=w_branch_a.astype(BF16),
        w_branch_b=w_branch_b.astype(BF16), w_out=w_out.astype(BF16), ln1_g=row(ln1_g),
        ln2_g=row(ln2_g), wq_hi=wq_hi, wq_lo=wq_lo, kblk=kblk, peer_u=peer_u.astype(BF16),
        peer_vt=peer_v.T.astype(BF16), final_g=row(final_g))


def kernel(x_prompt, x_sample, state_dn, state_dn_conv, state_lru_h, state_lru_conv, w_in, dn_conv_w, dn_a_log, dn_dt_bias, dn_norm_g, lru_conv_w, lru_conv_b, lru_w_r, lru_b_r, lru_w_i, lru_b_i, lru_lambda, w_branch_a, w_branch_b, w_out, ln1_g, ln2_g, peer_w_q, peer_keys, peer_u, peer_v, final_g):
    depth = w_in.shape[0]
    assert depth == 1, "the final norm is fused into the layer's last kernel"
    bp = x_prompt.shape[0]
    dt = x_prompt.dtype
    p = _prepare(w_in[0], dn_conv_w[0], dn_a_log[0], dn_dt_bias[0], dn_norm_g[0], lru_conv_w[0],
                 lru_conv_b[0], lru_w_r[0], lru_b_r[0], lru_w_i[0], lru_b_i[0], lru_lambda[0],
                 w_branch_a[0], w_branch_b[0], w_out[0], ln1_g[0], ln2_g[0], peer_w_q[0],
                 peer_keys[0], peer_u[0], peer_v[0], final_g)
    prompt = _stream(
        x_prompt,
        jnp.zeros((bp, DN_HEADS, DN_DK, DN_DV), dt),
        jnp.zeros((bp, CONV_W - 1, DN_CONV_CH), dt),
        jnp.zeros((bp, LRU_WIDTH), dt),
        jnp.zeros((bp, CONV_W - 1, LRU_WIDTH), dt),
        p)
    sample = _stream(x_sample, state_dn[0], state_dn_conv[0], state_lru_h[0], state_lru_conv[0], p)
    y_p, s_p, db_p, h_p, lb_p = prompt
    y_s, s_s, db_s, h_s, lb_s = sample
    return (y_p, y_s, s_p[None], db_p[None], h_p[None], lb_p[None],
            s_s[None], db_s[None], h_s[None], lb_s[None])
```

```python
import functools
import math

import jax
import jax.numpy as jnp
import numpy as np
from jax import lax
from jax.experimental import pallas as pl
from jax.experimental.pallas import tpu as pltpu

F32 = jnp.float32
BF16 = jnp.bfloat16

D_MODEL = 1024
CONV_W = 4
EPS = 1e-6
DN_HEADS = 8
DN_DK = 128
DN_DV = 128
DN_QK = DN_HEADS * DN_DK
DN_V = DN_HEADS * DN_DV
DN_CONV_CH = 2 * DN_QK + DN_V
LRU_WIDTH = D_MODEL
LRU_BLOCKS = 8
LRU_BDIM = LRU_WIDTH // LRU_BLOCKS
LRU_C = 8.0
PEER_HEADS = 8
PEER_NKEYS = 128
PEER_DK = 128
PEER_DKH = PEER_DK // 2
PEER_TOPK = 16
CHUNK = 64

LANES = 128
SUBLANES = 8
MAIN_COLS = 8192
PROJ_COLS = MAIN_COLS + LANES
PROJ_COL_TILE = PROJ_COLS // 5
BD_BLOCK = MAIN_COLS // LANES
TAIL_ROW = SUBLANES - (CONV_W - 1)
VMEM_LIMIT = 56 * 1024 * 1024
NEG_INF = float("-inf")


def _params(*sem):
    return pltpu.CompilerParams(dimension_semantics=sem, vmem_limit_bytes=VMEM_LIMIT)


def _split3(x):
    a = x.astype(BF16)
    r = x - a.astype(F32)
    b = r.astype(BF16)
    c = (r - b.astype(F32)).astype(BF16)
    return a, b, c


def _dot(a, b, dims=None):
    if dims is None:
        return jnp.dot(a, b, preferred_element_type=F32)
    return lax.dot_general(a, b, (dims, ((), ())), preferred_element_type=F32)


def _dot3(a, b, dims=None):
    ah = a.astype(BF16)
    al = (a - ah.astype(F32)).astype(BF16)
    bh = b.astype(BF16)
    bl = (b - bh.astype(F32)).astype(BF16)
    return _dot(ah, bh, dims) + (_dot(ah, bl, dims) + _dot(al, bh, dims))


def _dot_exact_lhs(a_bf16, b, dims=None):
    b0, b1, b2 = _split3(b)
    return _dot(a_bf16, b0, dims) + (_dot(a_bf16, b1, dims) + _dot(a_bf16, b2, dims))


def _sigmoid(x):
    return 1.0 / (1.0 + jnp.exp(-x))


def _softplus(x):
    return jnp.maximum(x, 0.0) + jnp.log1p(jnp.exp(-jnp.abs(x)))


def _rms_scale(x):
    return lax.rsqrt(jnp.mean(x * x, axis=-1, keepdims=True) + EPS)


def _causal_conv(xpad_ref, new_rows, w, rows):
    xpad_ref[SUBLANES:SUBLANES + rows, :] = new_rows
    xa = xpad_ref[...]
    y = None
    for j in range(CONV_W):
        delay = CONV_W - 1 - j
        shifted = pltpu.roll(xa, delay, 0) if delay else xa
        term = shifted[SUBLANES:SUBLANES + rows, :] * w[j:j + 1, :]
        y = term if y is None else y + term
    tail = xpad_ref[rows + TAIL_ROW:rows + SUBLANES, :]
    xpad_ref[TAIL_ROW:SUBLANES, :] = tail
    return y, tail


def _inproj_kernel(x_ref, g_ref, w_ref, o_ref):
    x = x_ref[...]
    n = (x * _rms_scale(x)) * g_ref[...]
    o_ref[...] = _dot(n.astype(BF16), w_ref[...])


def _in_proj(x, g, w_bf16, tm):
    n_tok = x.shape[0]
    return pl.pallas_call(
        _inproj_kernel,
        grid=(PROJ_COLS // PROJ_COL_TILE, n_tok // tm),
        in_specs=[
            pl.BlockSpec((tm, D_MODEL), lambda j, i: (i, 0)),
            pl.BlockSpec((1, D_MODEL), lambda j, i: (0, 0)),
            pl.BlockSpec((D_MODEL, PROJ_COL_TILE), lambda j, i: (0, j)),
        ],
        out_specs=pl.BlockSpec((tm, PROJ_COL_TILE), lambda j, i: (i, j)),
        out_shape=jax.ShapeDtypeStruct((n_tok, PROJ_COLS), F32),
        compiler_params=_params("arbitrary", "arbitrary"),
        name="in_proj",
    )(x, g, w_bf16)


def _bdot(a, b, eq):
    return jnp.einsum(eq, a, b, preferred_element_type=F32)


def _bdot1(a, b, eq):
    return _bdot(a.astype(BF16), b.astype(BF16), eq)


def _bdot3(a, b, eq):
    ah = a.astype(BF16)
    al = (a - ah.astype(F32)).astype(BF16)
    bh = b.astype(BF16)
    bl = (b - bh.astype(F32)).astype(BF16)
    return _bdot(ah, bh, eq) + (_bdot(ah, bl, eq) + _bdot(al, bh, eq))


MM = "gij,gjk->gik"
DN_CHUNKS_PER_STEP = 4


def _unit_lower_inverse(nmat, rows, row_i, col_i):
    eye = jnp.where(row_i == col_i, 1.0, 0.0)
    blk = SUBLANES
    differ = row_i ^ col_i
    nd = jnp.where(differ < blk, nmat, 0.0)
    t = eye - nd
    m = _bdot3(nd, nd, MM)
    t = t + _bdot3(t, m, MM)
    m = _bdot3(m, m, MM)
    t = t + _bdot3(t, m, MM)
    while blk < rows:
        off = jnp.where((differ >> int(math.log2(blk))) == 1, nmat, 0.0)
        t = t - _bdot3(t, _bdot3(off, t, MM), MM)
        blk *= 2
    return t


def _deltanet_kernel(qkv_ref, z_ref, bd_ref, s0_ref, buf0_ref, convw_ref, alog_ref, dtb_ref,
                     ng_ref, o_ref, s_ref, buf_ref, xpad_ref, *, rows, chunks):
    span = rows * chunks

    @pl.when(pl.program_id(1) == 0)
    def _():
        xpad_ref[TAIL_ROW:SUBLANES, :] = buf0_ref[0]
        s_ref[...] = s0_ref[...]

    y, tail = _causal_conv(xpad_ref, qkv_ref[0], convw_ref[...], span)
    buf_ref[0] = tail
    act = y * _sigmoid(y)

    bd = bd_ref[0]
    beta_all = _sigmoid(bd)
    g_all = -jnp.exp(alog_ref[...]) * _softplus(bd + dtb_ref[...])
    t_row = lax.broadcasted_iota(jnp.int32, (span, span), 0)
    t_col = lax.broadcasted_iota(jnp.int32, (span, span), 1)
    tri = jnp.where((t_row ^ t_col) < rows, jnp.where(t_row >= t_col, 1.0, 0.0), 0.0).astype(BF16)
    gcum = _dot_exact_lhs(tri, g_all)
    pad = -span % LANES
    gcum_t = (jnp.concatenate([gcum, jnp.zeros((pad, LANES), F32)], axis=0) if pad else gcum).T

    def per_head(src, col0, width):
        return jnp.stack([src[c * rows:(c + 1) * rows, col0 + h * width:col0 + (h + 1) * width]
                          for c in range(chunks) for h in range(DN_HEADS)])

    q = per_head(act, 0, DN_DK)
    k = per_head(act, DN_QK, DN_DK)
    v = per_head(act, 2 * DN_QK, DN_DV)
    z = per_head(z_ref[0], 0, DN_DV)
    beta = per_head(beta_all, 0, 1)
    gc = per_head(gcum, DN_HEADS, 1)
    gr = jnp.stack([gcum_t[DN_HEADS + h:DN_HEADS + h + 1, c * rows:(c + 1) * rows]
                    for c in range(chunks) for h in range(DN_HEADS)])
    g_last = gc[:, rows - 1:rows, :]

    q = q * (lax.rsqrt(jnp.sum(q * q, axis=-1, keepdims=True) + EPS) * (DN_DK ** -0.5))
    k = k * lax.rsqrt(jnp.sum(k * k, axis=-1, keepdims=True) + EPS)
    row_i = lax.broadcasted_iota(jnp.int32, (rows, rows), 0)
    col_i = lax.broadcasted_iota(jnp.int32, (rows, rows), 1)
    incl = row_i >= col_i
    dec = jnp.where(incl, jnp.exp(jnp.where(incl, gc - gr, 0.0)), 0.0)
    kb = k.astype(BF16)
    kk = _bdot(kb, kb, "gik,gjk->gij")
    nmat = jnp.where(row_i > col_i, beta * kk * dec, 0.0)
    gam = jnp.exp(gc)
    rhs = jnp.concatenate([(beta * gam) * k, beta * v], axis=-1)
    sol = _bdot1(_unit_lower_inverse(nmat, rows, row_i, col_i), rhs, MM)
    uv = sol[:, :, DN_DK:]
    wq = jnp.concatenate([sol[:, :, :DN_DK], q * gam], axis=1)
    qk = _bdot(q.astype(BF16), kb, "gik,gjk->gij") * dec
    kend = k * jnp.exp(g_last - gc)
    gend = jnp.exp(g_last)
    gate = ng_ref[...] * (z * _sigmoid(z))

    for c in range(chunks):
        hs = slice(c * DN_HEADS, (c + 1) * DN_HEADS)
        s_old = s_ref[0]
        ws = _bdot1(wq[hs], s_old, MM)
        u = uv[hs] - ws[:, :rows]
        ub = u.astype(BF16)
        o = ws[:, rows:] + _bdot(qk[hs].astype(BF16), ub, MM)
        s_ref[0] = gend[hs] * s_old + _bdot(kend[hs].astype(BF16), ub, "glk,glv->gkv")
        o = (o * _rms_scale(o)) * gate[hs]
        for h in range(DN_HEADS):
            o_ref[0, c * rows:(c + 1) * rows, h * DN_DV:(h + 1) * DN_DV] = o[h]


def _deltanet(proj3, s0, buf0, convw, alog_row, dtb_row, norm_g, rows, chunks):
    b, t, _ = proj3.shape
    span = rows * chunks
    nc = t // span
    rows, rows_per_chunk = span, rows
    return pl.pallas_call(
        functools.partial(_deltanet_kernel, rows=rows_per_chunk, chunks=chunks),
        grid=(b, nc),
        in_specs=[
            pl.BlockSpec((1, rows, DN_CONV_CH), lambda i, c: (i, c, 0)),
            pl.BlockSpec((1, rows, DN_V), lambda i, c: (i, c, DN_CONV_CH // DN_V)),
            pl.BlockSpec((1, rows, LANES), lambda i, c: (i, c, BD_BLOCK)),
            pl.BlockSpec((1, DN_HEADS, DN_DK, DN_DV), lambda i, c: (i, 0, 0, 0)),
            pl.BlockSpec((1, CONV_W - 1, DN_CONV_CH), lambda i, c: (i, 0, 0)),
            pl.BlockSpec((CONV_W, DN_CONV_CH), lambda i, c: (0, 0)),
            pl.BlockSpec((1, LANES), lambda i, c: (0, 0)),
            pl.BlockSpec((1, LANES), lambda i, c: (0, 0)),
            pl.BlockSpec((1, DN_DV), lambda i, c: (0, 0)),
        ],
        out_specs=[
            pl.BlockSpec((1, rows, DN_V), lambda i, c: (i, c, 0)),
            pl.BlockSpec((1, DN_HEADS, DN_DK, DN_DV), lambda i, c: (i, 0, 0, 0)),
            pl.BlockSpec((1, CONV_W - 1, DN_CONV_CH), lambda i, c: (i, 0, 0)),
        ],
        out_shape=[
            jax.ShapeDtypeStruct((b, t, DN_V), F32),
            jax.ShapeDtypeStruct((b, DN_HEADS, DN_DK, DN_DV), F32),
            jax.ShapeDtypeStruct((b, CONV_W - 1, DN_CONV_CH), F32),
        ],
        scratch_shapes=[pltpu.VMEM((SUBLANES + rows, DN_CONV_CH), F32)],
        compiler_params=_params("arbitrary", "arbitrary"),
        name="deltanet",
    )(proj3, proj3, proj3, s0, buf0, convw, alog_row, dtb_row, norm_g)


def _rglru_kernel(lx_ref, ly_ref, h0_ref, buf0_ref, convw_ref, convb_ref, wr_ref, br_ref,
                  wi_ref, bi_ref, lam_ref, o_ref, h_ref, buf_ref, xpad_ref, a_ref, b_ref, *, rows):
    c = pl.program_id(1)

    @pl.when(c == 0)
    def _():
        xpad_ref[TAIL_ROW:SUBLANES, :] = buf0_ref[0]
        h_ref[...] = h0_ref[...]

    y, tail = _causal_conv(xpad_ref, lx_ref[0], convw_ref[...], rows)
    buf_ref[0] = tail
    x = y + convb_ref[...]
    xb = x.astype(BF16)
    sp = LRU_C * _softplus(-lam_ref[...])
    for n in range(LRU_BLOCKS):
        lo = n * LRU_BDIM
        xn = xb[:, lo:lo + LRU_BDIM]
        r = _sigmoid(_dot(xn, wr_ref[n]) + br_ref[:, lo:lo + LRU_BDIM])
        i = _sigmoid(_dot(xn, wi_ref[n]) + bi_ref[:, lo:lo + LRU_BDIM])
        log_a = -(r * sp[:, lo:lo + LRU_BDIM])
        a = jnp.exp(log_a)
        a_ref[:, lo:lo + LRU_BDIM] = a
        one_minus_a2 = -jnp.tanh(log_a) * (a * a + 1.0)
        b_ref[:, lo:lo + LRU_BDIM] = jnp.sqrt(one_minus_a2) * (i * x[:, lo:lo + LRU_BDIM])

    groups = rows // SUBLANES
    a = a_ref[...].reshape(groups, SUBLANES, LRU_WIDTH)
    b = b_ref[...].reshape(groups, SUBLANES, LRU_WIDTH)
    sub = lax.broadcasted_iota(jnp.int32, (groups, SUBLANES, LRU_WIDTH), 1)
    d = 1
    while d < SUBLANES:
        keep = sub >= d
        a_prev = jnp.where(keep, pltpu.roll(a, d, 1), 1.0)
        b_prev = jnp.where(keep, pltpu.roll(b, d, 1), 0.0)
        b = a * b_prev + b
        a = a * a_prev
        d *= 2
    a_ref[...] = a.reshape(rows, LRU_WIDTH)
    b_ref[...] = b.reshape(rows, LRU_WIDTH)

    def group_step(g, h):
        r0 = pl.multiple_of(g * SUBLANES, SUBLANES)
        hs = a_ref[pl.ds(r0, SUBLANES), :] * h + b_ref[pl.ds(r0, SUBLANES), :]
        o_ref[0, pl.ds(r0, SUBLANES), :] = hs
        return hs[SUBLANES - 1:SUBLANES, :]

    h_ref[0] = lax.fori_loop(0, groups, group_step, h_ref[0], unroll=4)
    o_ref[0] = o_ref[0] * jax.nn.gelu(ly_ref[0], approximate=True)


def _rglru(proj3, h0, buf0, convw, convb, w_r, b_r, w_i, b_i, lam, rows):
    b, t, _ = proj3.shape
    nc = t // rows
    row = lambda i, c: (0, 0)
    return pl.pallas_call(
        functools.partial(_rglru_kernel, rows=rows),
        grid=(b, nc),
        in_specs=[
            pl.BlockSpec((1, rows, LRU_WIDTH), lambda i, c: (i, c, 4)),
            pl.BlockSpec((1, rows, LRU_WIDTH), lambda i, c: (i, c, 5)),
            pl.BlockSpec((1, 1, LRU_WIDTH), lambda i, c: (i, 0, 0)),
            pl.BlockSpec((1, CONV_W - 1, LRU_WIDTH), lambda i, c: (i, 0, 0)),
            pl.BlockSpec((CONV_W, LRU_WIDTH), row),
            pl.BlockSpec((1, LRU_WIDTH), row),
            pl.BlockSpec((LRU_BLOCKS, LRU_BDIM, LRU_BDIM), lambda i, c: (0, 0, 0)),
            pl.BlockSpec((1, LRU_WIDTH), row),
            pl.BlockSpec((LRU_BLOCKS, LRU_BDIM, LRU_BDIM), lambda i, c: (0, 0, 0)),
            pl.BlockSpec((1, LRU_WIDTH), row),
            pl.BlockSpec((1, LRU_WIDTH), row),
        ],
        out_specs=[
            pl.BlockSpec((1, rows, LRU_WIDTH), lambda i, c: (i, c, 0)),
            pl.BlockSpec((1, 1, LRU_WIDTH), lambda i, c: (i, 0, 0)),
            pl.BlockSpec((1, CONV_W - 1, LRU_WIDTH), lambda i, c: (i, 0, 0)),
        ],
        out_shape=[
            jax.ShapeDtypeStruct((b, t, LRU_WIDTH), F32),
            jax.ShapeDtypeStruct((b, 1, LRU_WIDTH), F32),
            jax.ShapeDtypeStruct((b, CONV_W - 1, LRU_WIDTH), F32),
        ],
        scratch_shapes=[
            pltpu.VMEM((SUBLANES + rows, LRU_WIDTH), F32),
            pltpu.VMEM((rows, LRU_WIDTH), F32),
            pltpu.VMEM((rows, LRU_WIDTH), F32),
        ],
        compiler_params=_params("arbitrary", "arbitrary"),
        name="rglru",
    )(proj3, proj3, h0, buf0, convw, convb, w_r, b_r, w_i, b_i, lam)


def _merge_kernel(x_ref, oa_ref, ob_ref, ga_ref, gb_ref, wa_ref, wb_ref, wo_ref, x1_ref):
    ya = _dot(oa_ref[...].astype(BF16), wa_ref[...])
    yb = _dot(ob_ref[...].astype(BF16), wb_ref[...])
    mix = _sigmoid(ga_ref[...]) * ya + _sigmoid(gb_ref[...]) * yb
    x1_ref[...] = x_ref[...] + _dot(mix.astype(BF16), wo_ref[...])


def _merge(x, o_a, o_b, proj, w_a, w_b, w_o, tm):
    n_tok = x.shape[0]
    tok = lambda i: (i, 0)
    full = lambda i: (0, 0)
    return pl.pallas_call(
        _merge_kernel,
        grid=(n_tok // tm,),
        in_specs=[
            pl.BlockSpec((tm, D_MODEL), tok),
            pl.BlockSpec((tm, DN_V), tok),
            pl.BlockSpec((tm, LRU_WIDTH), tok),
            pl.BlockSpec((tm, D_MODEL), lambda i: (i, 6)),
            pl.BlockSpec((tm, D_MODEL), lambda i: (i, 7)),
            pl.BlockSpec((DN_V, D_MODEL), full),
            pl.BlockSpec((LRU_WIDTH, D_MODEL), full),
            pl.BlockSpec((D_MODEL, D_MODEL), full),
        ],
        out_specs=pl.BlockSpec((tm, D_MODEL), tok),
        out_shape=jax.ShapeDtypeStruct((n_tok, D_MODEL), F32),
        compiler_params=_params("arbitrary"),
        name="merge",
    )(x, o_a, o_b, proj, proj, w_a, w_b, w_o)


def _candidate_slabs():
    return (
        ("row", 0, 0, 16, 0), ("row", 1, 0, 8, 0), ("row", 2, 0, 8, 0), ("row", 3, 0, 8, 0),
        ("col", 0, 8, 8, 0), ("col", 0, 0, 8, 4), ("col", 1, 0, 8, 4), ("col", 2, 0, 8, 4),
    )


def _candidate_positions():
    pos = []
    for kind, fixed, start, count, first_valid in _candidate_slabs():
        for r in range(start, start + count):
            i, j = (fixed, r) if kind == "row" else (r, fixed)
            masked = r - start < first_valid
            pos.append(PEER_TOPK * PEER_TOPK + len(pos) if masked else i * PEER_TOPK + j)
    assert len(set(pos)) == len(pos)
    return np.asarray(pos, np.float32).reshape(-1, 1)


MARK_BASE = -(2.0 ** 127 + 2.0 ** 126)
MARK_STEP = 2.0 ** 120
ORDER_BIG = float(1 << 20)
HEADS_PER_TRIP = 4


def _mark_top16(s, order):
    vals = []
    for r in range(PEER_TOPK):
        m = jnp.max(s, axis=1, keepdims=True)
        first = jnp.min(jnp.where(s == m, order, ORDER_BIG), axis=1, keepdims=True)
        s = jnp.where(order == first, MARK_BASE - r * MARK_STEP, s)
        vals.append(m)
    return s, jnp.concatenate(vals, axis=1)


def _is_marked(x):
    return (x <= MARK_BASE) & (x > MARK_BASE - PEER_TOPK * MARK_STEP)


def _rank_plus_half(marked):
    return jnp.where(_is_marked(marked), (MARK_BASE - marked) * (1.0 / MARK_STEP), float(PEER_TOPK)) + 0.5


def _retrieve_kernel(x1_ref, g_ref, wqh_ref, wql_ref, kblk_ref, pos_ref,
                     n2_ref, lena_ref, ca_ref, rankb_ref, pb_ref,
                     qt_ref, st_ref, v_ref, mark1_ref, *, tm):
    x1 = x1_ref[...]
    n2 = (x1 * _rms_scale(x1)) * g_ref[...]
    nh = n2.astype(BF16)
    nl = (n2 - nh.astype(F32)).astype(BF16)
    n2_ref[...] = nh
    nt = ((1,), (1,))
    qt_ref[...] = _dot(wqh_ref[...], nh, nt) + (_dot(wqh_ref[...], nl, nt) + _dot(wql_ref[...], nh, nt))

    key_iota = lax.broadcasted_iota(jnp.int32, (1, PEER_NKEYS, LANES), 1).astype(F32)
    pos = jnp.broadcast_to(pos_ref[...], (pos_ref.shape[0], LANES))[None]

    def head_group(hp, _):
        heads = [HEADS_PER_TRIP * hp + k for k in range(HEADS_PER_TRIP)]
        for h in heads:
            q = qt_ref[pl.ds(pl.multiple_of(h * PEER_DK, PEER_DK), PEER_DK), :]
            st_ref[h] = _dot3(kblk_ref[h], q).reshape(2, PEER_NKEYS, tm)
        for lb in range(tm // LANES):
            lanes = slice(lb * LANES, (lb + 1) * LANES)
            s = jnp.concatenate([st_ref[h, :, :, lanes] for h in heads], axis=0)
            marked, vals = _mark_top16(s, key_iota)
            for k, h in enumerate(heads):
                v_ref[h, :, :, lanes] = vals[2 * k:2 * k + 2]
                mark1_ref[h, :, lanes] = marked[2 * k]
                rankb_ref[h, :, lanes] = _rank_plus_half(marked[2 * k + 1]).astype(BF16)
                pb_ref[h, :, lanes] = jnp.exp(s[2 * k + 1] - vals[2 * k + 1, 0:1]).astype(BF16)
        return 0

    lax.fori_loop(0, PEER_HEADS // HEADS_PER_TRIP, head_group, 0)

    for lb in range(tm // LANES):
        lanes = slice(lb * LANES, (lb + 1) * LANES)
        v1 = v_ref[:, 0, :, lanes]
        v2 = v_ref[:, 1, :, lanes]
        pieces = []
        for kind, fixed, start, count, first_valid in _candidate_slabs():
            if kind == "row":
                piece = v1[:, fixed:fixed + 1] + v2[:, start:start + count]
            else:
                piece = v1[:, start:start + count] + v2[:, fixed:fixed + 1]
            if first_valid:
                local = lax.broadcasted_iota(jnp.int32, piece.shape, 1)
                piece = jnp.where(local >= first_valid, piece, NEG_INF)
            pieces.append(piece)
        cand = jnp.concatenate(pieces, axis=1)
        picked = _is_marked(_mark_top16(cand, pos)[0])
        t0 = v1[:, 0:1] + v2[:, 0:1]
        z = jnp.sum(jnp.where(picked, jnp.exp(cand - t0), 0.0), axis=1, keepdims=True)
        count_i = [None] * PEER_TOPK
        off = 0
        for kind, fixed, start, count, first_valid in _candidate_slabs():
            hit = jnp.where(picked[:, off:off + count], 1.0, 0.0)
            if kind == "row":
                add = [(fixed, jnp.sum(hit, axis=1, keepdims=True))]
            else:
                add = [(start + r, hit[:, r:r + 1]) for r in range(first_valid, count)]
            for i, term in add:
                count_i[i] = term if count_i[i] is None else count_i[i] + term
            off += count
        for h in range(PEER_HEADS):
            marked1 = mark1_ref[h, :, lanes]
            len_a = jnp.zeros((PEER_NKEYS, LANES), F32)
            for i in range(PEER_TOPK):
                len_a = jnp.where(marked1 == MARK_BASE - i * MARK_STEP, count_i[i][h], len_a)
            lena_ref[h, :, lanes] = len_a
            ca_ref[h, :, lanes] = jnp.exp(st_ref[h, 0, :, lanes] - v1[h, 0:1]) / z[h]


def _retrieve(x1, g2, wq_hi, wq_lo, kblk, tm):
    n_tok = x1.shape[0]
    assert tm % LANES == 0, "selection runs on whole lane groups of tokens"
    pos = jnp.asarray(_candidate_positions())
    key_spec = pl.BlockSpec((PEER_HEADS, PEER_NKEYS, tm), lambda i: (0, 0, i))
    per_key = lambda dtype: jax.ShapeDtypeStruct((PEER_HEADS, PEER_NKEYS, n_tok), dtype)
    return pl.pallas_call(
        functools.partial(_retrieve_kernel, tm=tm),
        grid=(n_tok // tm,),
        in_specs=[
            pl.BlockSpec((tm, D_MODEL), lambda i: (i, 0)),
            pl.BlockSpec((1, D_MODEL), lambda i: (0, 0)),
            pl.BlockSpec((D_MODEL, D_MODEL), lambda i: (0, 0)),
            pl.BlockSpec((D_MODEL, D_MODEL), lambda i: (0, 0)),
            pl.BlockSpec((PEER_HEADS, 2 * PEER_NKEYS, PEER_DK), lambda i: (0, 0, 0)),
            pl.BlockSpec(pos.shape, lambda i: (0, 0)),
        ],
        out_specs=[pl.BlockSpec((tm, D_MODEL), lambda i: (i, 0)), key_spec, key_spec, key_spec, key_spec],
        out_shape=[jax.ShapeDtypeStruct((n_tok, D_MODEL), BF16), per_key(F32), per_key(F32),
                   per_key(BF16), per_key(BF16)],
        scratch_shapes=[
            pltpu.VMEM((D_MODEL, tm), F32),
            pltpu.VMEM((PEER_HEADS, 2, PEER_NKEYS, tm), F32),
            pltpu.VMEM((PEER_HEADS, 2, PEER_TOPK, tm), F32),
            pltpu.VMEM((PEER_HEADS, PEER_NKEYS, tm), F32),
        ],
        compiler_params=_params("arbitrary"),
        name="retrieve",
    )(x1, g2, wq_hi, wq_lo, kblk, pos)


EXPERT_TILE = 1024
A_PER_TILE = EXPERT_TILE // PEER_NKEYS


def _experts_kernel(x1_ref, n2_ref, u_ref, vt_ref, lena_ref, ca_ref, rankb_ref, pb_ref, fg_ref,
                    y_ref, acc_ref, *, tm):
    j = pl.program_id(1)

    @pl.when(j == 0)
    def _():
        acc_ref[...] = jnp.zeros_like(acc_ref)

    s = _dot(u_ref[...], n2_ref[...], ((1,), (1,)))
    pack = 2 * SUBLANES
    groups = PEER_NKEYS // pack
    acts = []
    for al in range(A_PER_TILE):
        w = None
        for h in range(PEER_HEADS):
            len_a = jnp.broadcast_to(lena_ref[h, al:al + 1, :], (pack, tm)).astype(BF16)
            c_a = jnp.broadcast_to(ca_ref[h, al:al + 1, :], (pack, tm)).astype(BF16)
            rank_b = rankb_ref[h].reshape(groups, pack, tm)
            p_b = pb_ref[h].reshape(groups, pack, tm)
            t = jnp.where(rank_b < len_a[None], p_b, jnp.zeros_like(p_b)) * c_a[None]
            w = t if w is None else w + t
        sa = s[al * PEER_NKEYS:(al + 1) * PEER_NKEYS]
        act = 0.5 * sa * (1.0 + lax.erf(sa * (1.0 / math.sqrt(2.0))))
        acts.append(act.astype(BF16) * w.reshape(PEER_NKEYS, tm))
    acc_ref[...] += _dot(vt_ref[...], jnp.concatenate(acts, axis=0))

    @pl.when(j == pl.num_programs(1) - 1)
    def _():
        x2 = x1_ref[...] + acc_ref[...].T
        y_ref[...] = (x2 * _rms_scale(x2)) * fg_ref[...]


def _experts(x1, n2, u_bf16, vt_bf16, len_a, c_a, rank_b, p_b, final_g, tm):
    n_tok = x1.shape[0]
    n_exp = u_bf16.shape[0]
    a_spec = pl.BlockSpec((PEER_HEADS, A_PER_TILE, tm), lambda i, j: (0, j, i))
    b_spec = pl.BlockSpec((PEER_HEADS, PEER_NKEYS, tm), lambda i, j: (0, 0, i))
    return pl.pallas_call(
        functools.partial(_experts_kernel, tm=tm),
        grid=(n_tok // tm, n_exp // EXPERT_TILE),
        in_specs=[
            pl.BlockSpec((tm, D_MODEL), lambda i, j: (i, 0)),
            pl.BlockSpec((tm, D_MODEL), lambda i, j: (i, 0)),
            pl.BlockSpec((EXPERT_TILE, D_MODEL), lambda i, j: (j, 0)),
            pl.BlockSpec((D_MODEL, EXPERT_TILE), lambda i, j: (0, j)),
            a_spec, a_spec, b_spec, b_spec,
            pl.BlockSpec((1, D_MODEL), lambda i, j: (0, 0)),
        ],
        out_specs=pl.BlockSpec((tm, D_MODEL), lambda i, j: (i, 0)),
        out_shape=jax.ShapeDtypeStruct((n_tok, D_MODEL), F32),
        scratch_shapes=[pltpu.VMEM((D_MODEL, tm), F32)],
        compiler_params=_params("arbitrary", "arbitrary"),
        name="experts",
    )(x1, n2, u_bf16, vt_bf16, len_a, c_a, rank_b, p_b, final_g)


def _token_tile(n_tok, cap):
    tm = cap
    while n_tok % tm:
        tm //= 2
    return tm


def _stream(x, s0, dn_buf0, h0, lru_buf0, p):
    b, t, _ = x.shape
    n_tok = b * t
    rows = min(CHUNK, t)
    xf = x.reshape(n_tok, D_MODEL)
    proj = _in_proj(xf, p["ln1_g"], p["w_in"], _token_tile(n_tok, 1024))
    proj3 = proj.reshape(b, t, PROJ_COLS)
    o_a, s_new, dn_buf = _deltanet(proj3, s0, dn_buf0, p["dn_conv_w"], p["alog_row"], p["dtb_row"],
                                   p["dn_norm_g"], rows, min(DN_CHUNKS_PER_STEP, t // rows))
    lru_rows = _token_tile(t, 256)
    o_b, h_new, lru_buf = _rglru(proj3, h0.reshape(b, 1, LRU_WIDTH), lru_buf0, p["lru_conv_w"],
                                 p["lru_conv_b"], p["lru_w_r"], p["lru_b_r"], p["lru_w_i"],
                                 p["lru_b_i"], p["lru_lambda"], lru_rows)
    x1 = _merge(xf, o_a.reshape(n_tok, DN_V), o_b.reshape(n_tok, LRU_WIDTH), proj,
                p["w_branch_a"], p["w_branch_b"], p["w_out"], _token_tile(n_tok, 512))
    tr = _token_tile(n_tok, 256)
    n2, len_a, c_a, rank_b, p_b = _retrieve(x1, p["ln2_g"], p["wq_hi"], p["wq_lo"], p["kblk"], tr)
    y = _experts(x1, n2, p["peer_u"], p["peer_vt"], len_a, c_a, rank_b, p_b, p["final_g"],
                 _token_tile(n_tok, 1024))
    return (y.reshape(b, t, D_MODEL), s_new, dn_buf, h_new.reshape(b, LRU_WIDTH), lru_buf)


def _prepare(w_in, dn_conv_w, dn_a_log, dn_dt_bias, dn_norm_g, lru_conv_w, lru_conv_b, lru_w_r,
             lru_b_r, lru_w_i, lru_b_i, lru_lambda, w_branch_a, w_branch_b, w_out, ln1_g, ln2_g,
             peer_w_q, peer_keys, peer_u, peer_v, final_g):
    row = lambda v: v.reshape(1, -1).astype(F32)
    lane_row = lambda v: jnp.zeros((1, LANES), F32).at[0, DN_HEADS:2 * DN_HEADS].set(v)
    w_in_p = jnp.pad(w_in, ((0, 0), (0, PROJ_COLS - w_in.shape[1]))).astype(BF16)
    wq_t = peer_w_q.T
    wq_hi = wq_t.astype(BF16)
    wq_lo = (wq_t - wq_hi.astype(F32)).astype(BF16)
    zeros = jnp.zeros((PEER_HEADS, PEER_NKEYS, PEER_DKH), F32)
    kblk = jnp.concatenate([
        jnp.concatenate([peer_keys[:, 0], zeros], axis=-1),
        jnp.concatenate([zeros, peer_keys[:, 1]], axis=-1)], axis=1)
    return dict(
        w_in=w_in_p, dn_conv_w=dn_conv_w, alog_row=lane_row(dn_a_log), dtb_row=lane_row(dn_dt_bias),
        dn_norm_g=row(dn_norm_g), lru_conv_w=lru_conv_w, lru_conv_b=row(lru_conv_b),
        lru_w_r=lru_w_r.astype(BF16), lru_b_r=row(lru_b_r), lru_w_i=lru_w_i.astype(BF16),
        lru_b_i=row(lru_b_i), lru_lambda=row(lru_lambda), w_branch_a=w_branch_a.astype(BF16),
        w_branch_b=w_branch_b.astype(BF16), w_out=w_out.astype(BF16), ln1_g=row(ln1_g),
        ln2_g=row(ln2_g), wq_hi=wq_hi, wq_lo=wq_lo, kblk=kblk, peer_u=peer_u.astype(BF16),
        peer_vt=peer_v.T.astype(BF16), final_g=row(final_g))


def kernel(x_prompt, x_sample, state_dn, state_dn_conv, state_lru_h, state_lru_conv, w_in, dn_conv_w, dn_a_log, dn_dt_bias, dn_norm_g, lru_conv_w, lru_conv_b, lru_w_r, lru_b_r, lru_w_i, lru_b_i, lru_lambda, w_branch_a, w_branch_b, w_out, ln1_g, ln2_g, peer_w_q, peer_keys, peer_u, peer_v, final_g):
    depth = w_in.shape[0]
    assert depth == 1, "the final norm is fused into the layer's last kernel"
    bp = x_prompt.shape[0]
    dt = x_prompt.dtype
    p = _prepare(w_in[0], dn_conv_w[0], dn_a_log[0], dn_dt_bias[0], dn_norm_g[0], lru_conv_w[0],
                 lru_conv_b[0], lru_w_r[0], lru_b_r[0], lru_w_i[0], lru_b_i[0], lru_lambda[0],
                 w_branch_a[0], w_branch_b[0], w_out[0], ln1_g[0], ln2_g[0], peer_w_q[0],
                 peer_keys[0], peer_u[0], peer_v[0], final_g)
    prompt = _stream(
        x_prompt,
        jnp.zeros((bp, DN_HEADS, DN_DK, DN_DV), dt),
        jnp.zeros((bp, CONV_W - 1, DN_CONV_CH), dt),
        jnp.zeros((bp, LRU_WIDTH), dt),
        jnp.zeros((bp, CONV_W - 1, LRU_WIDTH), dt),
        p)
    sample = _stream(x_sample, state_dn[0], state_dn_conv[0], state_lru_h[0], state_lru_conv[0], p)
    y_p, s_p, db_p, h_p, lb_p = prompt
    y_s, s_s, db_s, h_s, lb_s = sample
    return (y_p, y_s, s_p[None], db_p[None], h_p[None], lb_p[None],
            s_s[None], db_s[None], h_s[None], lb_s[None])
```

```python
import functools
import math

import jax
import jax.numpy as jnp
import numpy as np
from jax import lax
from jax.experimental import pallas as pl
from jax.experimental.pallas import tpu as pltpu

F32 = jnp.float32
BF16 = jnp.bfloat16

D_MODEL = 1024
CONV_W = 4
EPS = 1e-6
DN_HEADS = 8
DN_DK = 128
DN_DV = 128
DN_QK = DN_HEADS * DN_DK
DN_V = DN_HEADS * DN_DV
DN_CONV_CH = 2 * DN_QK + DN_V
LRU_WIDTH = D_MODEL
LRU_BLOCKS = 8
LRU_BDIM = LRU_WIDTH // LRU_BLOCKS
LRU_C = 8.0
PEER_HEADS = 8
PEER_NKEYS = 128
PEER_DK = 128
PEER_DKH = PEER_DK // 2
PEER_TOPK = 16
CHUNK = 64

LANES = 128
SUBLANES = 8
MAIN_COLS = 8192
PROJ_COLS = MAIN_COLS + LANES
PROJ_COL_TILE = PROJ_COLS // 5
BD_BLOCK = MAIN_COLS // LANES
TAIL_ROW = SUBLANES - (CONV_W - 1)
VMEM_LIMIT = 56 * 1024 * 1024
NEG_INF = float("-inf")


def _params(*sem):
    return pltpu.CompilerParams(dimension_semantics=sem, vmem_limit_bytes=VMEM_LIMIT)


def _split3(x):
    a = x.astype(BF16)
    r = x - a.astype(F32)
    b = r.astype(BF16)
    c = (r - b.astype(F32)).astype(BF16)
    return a, b, c


def _dot(a, b, dims=None):
    if dims is None:
        return jnp.dot(a, b, preferred_element_type=F32)
    return lax.dot_general(a, b, (dims, ((), ())), preferred_element_type=F32)


def _dot3(a, b, dims=None):
    ah = a.astype(BF16)
    al = (a - ah.astype(F32)).astype(BF16)
    bh = b.astype(BF16)
    bl = (b - bh.astype(F32)).astype(BF16)
    return _dot(ah, bh, dims) + (_dot(ah, bl, dims) + _dot(al, bh, dims))


def _dot_exact_lhs(a_bf16, b, dims=None):
    b0, b1, b2 = _split3(b)
    return _dot(a_bf16, b0, dims) + (_dot(a_bf16, b1, dims) + _dot(a_bf16, b2, dims))


def _sigmoid(x):
    return 1.0 / (1.0 + jnp.exp(-x))


def _softplus(x):
    return jnp.maximum(x, 0.0) + jnp.log1p(jnp.exp(-jnp.abs(x)))


def _rms_scale(x):
    return lax.rsqrt(jnp.mean(x * x, axis=-1, keepdims=True) + EPS)


def _causal_conv(xpad_ref, new_rows, w, rows):
    xpad_ref[SUBLANES:SUBLANES + rows, :] = new_rows
    xa = xpad_ref[...]
    y = None
    for j in range(CONV_W):
        delay = CONV_W - 1 - j
        shifted = pltpu.roll(xa, delay, 0) if delay else xa
        term = shifted[SUBLANES:SUBLANES + rows, :] * w[j:j + 1, :]
        y = term if y is None else y + term
    tail = xpad_ref[rows + TAIL_ROW:rows + SUBLANES, :]
    xpad_ref[TAIL_ROW:SUBLANES, :] = tail
    return y, tail


def _inproj_kernel(x_ref, g_ref, w_ref, o_ref):
    x = x_ref[...]
    n = (x * _rms_scale(x)) * g_ref[...]
    o_ref[...] = _dot(n.astype(BF16), w_ref[...])


def _in_proj(x, g, w_bf16, tm):
    n_tok = x.shape[0]
    return pl.pallas_call(
        _inproj_kernel,
        grid=(PROJ_COLS // PROJ_COL_TILE, n_tok // tm),
        in_specs=[
            pl.BlockSpec((tm, D_MODEL), lambda j, i: (i, 0)),
            pl.BlockSpec((1, D_MODEL), lambda j, i: (0, 0)),
            pl.BlockSpec((D_MODEL, PROJ_COL_TILE), lambda j, i: (0, j)),
        ],
        out_specs=pl.BlockSpec((tm, PROJ_COL_TILE), lambda j, i: (i, j)),
        out_shape=jax.ShapeDtypeStruct((n_tok, PROJ_COLS), F32),
        compiler_params=_params("arbitrary", "arbitrary"),
        name="in_proj",
    )(x, g, w_bf16)


def _bdot(a, b, eq):
    return jnp.einsum(eq, a, b, preferred_element_type=F32)


def _bdot1(a, b, eq):
    return _bdot(a.astype(BF16), b.astype(BF16), eq)


def _bdot3(a, b, eq):
    ah = a.astype(BF16)
    al = (a - ah.astype(F32)).astype(BF16)
    bh = b.astype(BF16)
    bl = (b - bh.astype(F32)).astype(BF16)
    return _bdot(ah, bh, eq) + (_bdot(ah, bl, eq) + _bdot(al, bh, eq))


MM = "gij,gjk->gik"
DN_CHUNKS_PER_STEP = 4


def _unit_lower_inverse(nmat, rows, row_i, col_i):
    eye = jnp.where(row_i == col_i, 1.0, 0.0)
    blk = SUBLANES
    differ = row_i ^ col_i
    nd = jnp.where(differ < blk, nmat, 0.0)
    t = eye - nd
    m = _bdot3(nd, nd, MM)
    t = t + _bdot3(t, m, MM)
    m = _bdot3(m, m, MM)
    t = t + _bdot3(t, m, MM)
    while blk < rows:
        off = jnp.where((differ >> int(math.log2(blk))) == 1, nmat, 0.0)
        t = t - _bdot3(t, _bdot3(off, t, MM), MM)
        blk *= 2
    return t


def _deltanet_kernel(qkv_ref, z_ref, bd_ref, s0_ref, buf0_ref, convw_ref, alog_ref, dtb_ref,
                     ng_ref, o_ref, s_ref, buf_ref, xpad_ref, *, rows, chunks):
    span = rows * chunks

    @pl.when(pl.program_id(1) == 0)
    def _():
        xpad_ref[TAIL_ROW:SUBLANES, :] = buf0_ref[0]
        s_ref[...] = s0_ref[...]

    y, tail = _causal_conv(xpad_ref, qkv_ref[0], convw_ref[...], span)
    buf_ref[0] = tail
    act = y * _sigmoid(y)

    bd = bd_ref[0]
    beta_all = _sigmoid(bd)
    g_all = -jnp.exp(alog_ref[...]) * _softplus(bd + dtb_ref[...])
    t_row = lax.broadcasted_iota(jnp.int32, (span, span), 0)
    t_col = lax.broadcasted_iota(jnp.int32, (span, span), 1)
    tri = jnp.where((t_row ^ t_col) < rows, jnp.where(t_row >= t_col, 1.0, 0.0), 0.0).astype(BF16)
    gcum = _dot_exact_lhs(tri, g_all)
    pad = -span % LANES
    gcum_t = (jnp.concatenate([gcum, jnp.zeros((pad, LANES), F32)], axis=0) if pad else gcum).T

    def per_head(src, col0, width):
        return jnp.stack([src[c * rows:(c + 1) * rows, col0 + h * width:col0 + (h + 1) * width]
                          for c in range(chunks) for h in range(DN_HEADS)])

    q = per_head(act, 0, DN_DK)
    k = per_head(act, DN_QK, DN_DK)
    v = per_head(act, 2 * DN_QK, DN_DV)
    z = per_head(z_ref[0], 0, DN_DV)
    beta = per_head(beta_all, 0, 1)
    gc = per_head(gcum, DN_HEADS, 1)
    gr = jnp.stack([gcum_t[DN_HEADS + h:DN_HEADS + h + 1, c * rows:(c + 1) * rows]
                    for c in range(chunks) for h in range(DN_HEADS)])
    g_last = gc[:, rows - 1:rows, :]

    q = q * (lax.rsqrt(jnp.sum(q * q, axis=-1, keepdims=True) + EPS) * (DN_DK ** -0.5))
    k = k * lax.rsqrt(jnp.sum(k * k, axis=-1, keepdims=True) + EPS)
    row_i = lax.broadcasted_iota(jnp.int32, (rows, rows), 0)
    col_i = lax.broadcasted_iota(jnp.int32, (rows, rows), 1)
    incl = row_i >= col_i
    dec = jnp.where(incl, jnp.exp(jnp.where(incl, gc - gr, 0.0)), 0.0)
    kb = k.astype(BF16)
    kk = _bdot(kb, kb, "gik,gjk->gij")
    nmat = jnp.where(row_i > col_i, beta * kk * dec, 0.0)
    gam = jnp.exp(gc)
    rhs = jnp.concatenate([(beta * gam) * k, beta * v], axis=-1)
    sol = _bdot1(_unit_lower_inverse(nmat, rows, row_i, col_i), rhs, MM)
    uv = sol[:, :, DN_DK:]
    wq = jnp.concatenate([sol[:, :, :DN_DK], q * gam], axis=1)
    qk = _bdot(q.astype(BF16), kb, "gik,gjk->gij") * dec
    kend = k * jnp.exp(g_last - gc)
    gend = jnp.exp(g_last)
    gate = ng_ref[...] * (z * _sigmoid(z))

    for c in range(chunks):
        hs = slice(c * DN_HEADS, (c + 1) * DN_HEADS)
        s_old = s_ref[0]
        ws = _bdot1(wq[hs], s_old, MM)
        u = uv[hs] - ws[:, :rows]
        ub = u.astype(BF16)
        o = ws[:, rows:] + _bdot(qk[hs].astype(BF16), ub, MM)
        s_ref[0] = gend[hs] * s_old + _bdot(kend[hs].astype(BF16), ub, "glk,glv->gkv")
        o = (o * _rms_scale(o)) * gate[hs]
        for h in range(DN_HEADS):
            o_ref[0, c * rows:(c + 1) * rows, h * DN_DV:(h + 1) * DN_DV] = o[h]


def _deltanet(proj3, s0, buf0, convw, alog_row, dtb_row, norm_g, rows, chunks):
    b, t, _ = proj3.shape
    span = rows * chunks
    nc = t // span
    rows, rows_per_chunk = span, rows
    return pl.pallas_call(
        functools.partial(_deltanet_kernel, rows=rows_per_chunk, chunks=chunks),
        grid=(b, nc),
        in_specs=[
            pl.BlockSpec((1, rows, DN_CONV_CH), lambda i, c: (i, c, 0)),
            pl.BlockSpec((1, rows, DN_V), lambda i, c: (i, c, DN_CONV_CH // DN_V)),
            pl.BlockSpec((1, rows, LANES), lambda i, c: (i, c, BD_BLOCK)),
            pl.BlockSpec((1, DN_HEADS, DN_DK, DN_DV), lambda i, c: (i, 0, 0, 0)),
            pl.BlockSpec((1, CONV_W - 1, DN_CONV_CH), lambda i, c: (i, 0, 0)),
            pl.BlockSpec((CONV_W, DN_CONV_CH), lambda i, c: (0, 0)),
            pl.BlockSpec((1, LANES), lambda i, c: (0, 0)),
            pl.BlockSpec((1, LANES), lambda i, c: (0, 0)),
            pl.BlockSpec((1, DN_DV), lambda i, c: (0, 0)),
        ],
        out_specs=[
            pl.BlockSpec((1, rows, DN_V), lambda i, c: (i, c, 0)),
            pl.BlockSpec((1, DN_HEADS, DN_DK, DN_DV), lambda i, c: (i, 0, 0, 0)),
            pl.BlockSpec((1, CONV_W - 1, DN_CONV_CH), lambda i, c: (i, 0, 0)),
        ],
        out_shape=[
            jax.ShapeDtypeStruct((b, t, DN_V), F32),
            jax.ShapeDtypeStruct((b, DN_HEADS, DN_DK, DN_DV), F32),
            jax.ShapeDtypeStruct((b, CONV_W - 1, DN_CONV_CH), F32),
        ],
        scratch_shapes=[pltpu.VMEM((SUBLANES + rows, DN_CONV_CH), F32)],
        compiler_params=_params("arbitrary", "arbitrary"),
        name="deltanet",
    )(proj3, proj3, proj3, s0, buf0, convw, alog_row, dtb_row, norm_g)


def _rglru_kernel(lx_ref, ly_ref, h0_ref, buf0_ref, convw_ref, convb_ref, wr_ref, br_ref,
                  wi_ref, bi_ref, lam_ref, o_ref, h_ref, buf_ref, xpad_ref, a_ref, b_ref, *, rows):
    c = pl.program_id(1)

    @pl.when(c == 0)
    def _():
        xpad_ref[TAIL_ROW:SUBLANES, :] = buf0_ref[0]
        h_ref[...] = h0_ref[...]

    y, tail = _causal_conv(xpad_ref, lx_ref[0], convw_ref[...], rows)
    buf_ref[0] = tail
    x = y + convb_ref[...]
    xb = x.astype(BF16)
    sp = LRU_C * _softplus(-lam_ref[...])
    for n in range(LRU_BLOCKS):
        lo = n * LRU_BDIM
        xn = xb[:, lo:lo + LRU_BDIM]
        r = _sigmoid(_dot(xn, wr_ref[n]) + br_ref[:, lo:lo + LRU_BDIM])
        i = _sigmoid(_dot(xn, wi_ref[n]) + bi_ref[:, lo:lo + LRU_BDIM])
        log_a = -(r * sp[:, lo:lo + LRU_BDIM])
        a = jnp.exp(log_a)
        a_ref[:, lo:lo + LRU_BDIM] = a
        one_minus_a2 = -jnp.tanh(log_a) * (a * a + 1.0)
        b_ref[:, lo:lo + LRU_BDIM] = jnp.sqrt(one_minus_a2) * (i * x[:, lo:lo + LRU_BDIM])

    groups = rows // SUBLANES
    a = a_ref[...].reshape(groups, SUBLANES, LRU_WIDTH)
    b = b_ref[...].reshape(groups, SUBLANES, LRU_WIDTH)
    sub = lax.broadcasted_iota(jnp.int32, (groups, SUBLANES, LRU_WIDTH), 1)
    d = 1
    while d < SUBLANES:
        keep = sub >= d
        a_prev = jnp.where(keep, pltpu.roll(a, d, 1), 1.0)
        b_prev = jnp.where(keep, pltpu.roll(b, d, 1), 0.0)
        b = a * b_prev + b
        a = a * a_prev
        d *= 2
    a_ref[...] = a.reshape(rows, LRU_WIDTH)
    b_ref[...] = b.reshape(rows, LRU_WIDTH)

    def group_step(g, h):
        r0 = pl.multiple_of(g * SUBLANES, SUBLANES)
        hs = a_ref[pl.ds(r0, SUBLANES), :] * h + b_ref[pl.ds(r0, SUBLANES), :]
        o_ref[0, pl.ds(r0, SUBLANES), :] = hs
        return hs[SUBLANES - 1:SUBLANES, :]

    h_ref[0] = lax.fori_loop(0, groups, group_step, h_ref[0], unroll=4)
    o_ref[0] = o_ref[0] * jax.nn.gelu(ly_ref[0], approximate=True)


def _rglru(proj3, h0, buf0, convw, convb, w_r, b_r, w_i, b_i, lam, rows):
    b, t, _ = proj3.shape
    nc = t // rows
    row = lambda i, c: (0, 0)
    return pl.pallas_call(
        functools.partial(_rglru_kernel, rows=rows),
        grid=(b, nc),
        in_specs=[
            pl.BlockSpec((1, rows, LRU_WIDTH), lambda i, c: (i, c, 4)),
            pl.BlockSpec((1, rows, LRU_WIDTH), lambda i, c: (i, c, 5)),
            pl.BlockSpec((1, 1, LRU_WIDTH), lambda i, c: (i, 0, 0)),
            pl.BlockSpec((1, CONV_W - 1, LRU_WIDTH), lambda i, c: (i, 0, 0)),
            pl.BlockSpec((CONV_W, LRU_WIDTH), row),
            pl.BlockSpec((1, LRU_WIDTH), row),
            pl.BlockSpec((LRU_BLOCKS, LRU_BDIM, LRU_BDIM), lambda i, c: (0, 0, 0)),
            pl.BlockSpec((1, LRU_WIDTH), row),
            pl.BlockSpec((LRU_BLOCKS, LRU_BDIM, LRU_BDIM), lambda i, c: (0, 0, 0)),
            pl.BlockSpec((1, LRU_WIDTH), row),
            pl.BlockSpec((1, LRU_WIDTH), row),
        ],
        out_specs=[
            pl.BlockSpec((1, rows, LRU_WIDTH), lambda i, c: (i, c, 0)),
            pl.BlockSpec((1, 1, LRU_WIDTH), lambda i, c: (i, 0, 0)),
            pl.BlockSpec((1, CONV_W - 1, LRU_WIDTH), lambda i, c: (i, 0, 0)),
        ],
        out_shape=[
            jax.ShapeDtypeStruct((b, t, LRU_WIDTH), F32),
            jax.ShapeDtypeStruct((b, 1, LRU_WIDTH), F32),
            jax.ShapeDtypeStruct((b, CONV_W - 1, LRU_WIDTH), F32),
        ],
        scratch_shapes=[
            pltpu.VMEM((SUBLANES + rows, LRU_WIDTH), F32),
            pltpu.VMEM((rows, LRU_WIDTH), F32),
            pltpu.VMEM((rows, LRU_WIDTH), F32),
        ],
        compiler_params=_params("arbitrary", "arbitrary"),
        name="rglru",
    )(proj3, proj3, h0, buf0, convw, convb, w_r, b_r, w_i, b_i, lam)


def _merge_kernel(x_ref, oa_ref, ob_ref, ga_ref, gb_ref, wa_ref, wb_ref, wo_ref, x1_ref):
    ya = _dot(oa_ref[...].astype(BF16), wa_ref[...])
    yb = _dot(ob_ref[...].astype(BF16), wb_ref[...])
    mix = _sigmoid(ga_ref[...]) * ya + _sigmoid(gb_ref[...]) * yb
    x1_ref[...] = x_ref[...] + _dot(mix.astype(BF16), wo_ref[...])


def _merge(x, o_a, o_b, proj, w_a, w_b, w_o, tm):
    n_tok = x.shape[0]
    tok = lambda i: (i, 0)
    full = lambda i: (0, 0)
    return pl.pallas_call(
        _merge_kernel,
        grid=(n_tok // tm,),
        in_specs=[
            pl.BlockSpec((tm, D_MODEL), tok),
            pl.BlockSpec((tm, DN_V), tok),
            pl.BlockSpec((tm, LRU_WIDTH), tok),
            pl.BlockSpec((tm, D_MODEL), lambda i: (i, 6)),
            pl.BlockSpec((tm, D_MODEL), lambda i: (i, 7)),
            pl.BlockSpec((DN_V, D_MODEL), full),
            pl.BlockSpec((LRU_WIDTH, D_MODEL), full),
            pl.BlockSpec((D_MODEL, D_MODEL), full),
        ],
        out_specs=pl.BlockSpec((tm, D_MODEL), tok),
        out_shape=jax.ShapeDtypeStruct((n_tok, D_MODEL), F32),
        compiler_params=_params("arbitrary"),
        name="merge",
    )(x, o_a, o_b, proj, proj, w_a, w_b, w_o)


def _candidate_slabs():
    return (
        ("row", 0, 0, 16, 0), ("row", 1, 0, 8, 0), ("row", 2, 0, 8, 0), ("row", 3, 0, 8, 0),
        ("col", 0, 8, 8, 0), ("col", 0, 0, 8, 4), ("col", 1, 0, 8, 4), ("col", 2, 0, 8, 4),
    )


def _candidate_positions():
    pos = []
    for kind, fixed, start, count, first_valid in _candidate_slabs():
        for r in range(start, start + count):
            i, j = (fixed, r) if kind == "row" else (r, fixed)
            masked = r - start < first_valid
            pos.append(PEER_TOPK * PEER_TOPK + len(pos) if masked else i * PEER_TOPK + j)
    assert len(set(pos)) == len(pos)
    return np.asarray(pos, np.float32).reshape(-1, 1)


MARK_BASE = -(2.0 ** 127 + 2.0 ** 126)
MARK_STEP = 2.0 ** 120
ORDER_BIG = float(1 << 20)
HEADS_PER_TRIP = 4


def _mark_top16(s, order):
    vals = []
    for r in range(PEER_TOPK):
        m = jnp.max(s, axis=1, keepdims=True)
        first = jnp.min(jnp.where(s == m, order, ORDER_BIG), axis=1, keepdims=True)
        s = jnp.where(order == first, MARK_BASE - r * MARK_STEP, s)
        vals.append(m)
    return s, jnp.concatenate(vals, axis=1)


def _is_marked(x):
    return (x <= MARK_BASE) & (x > MARK_BASE - PEER_TOPK * MARK_STEP)


def _rank_plus_half(marked):
    return jnp.where(_is_marked(marked), (MARK_BASE - marked) * (1.0 / MARK_STEP), float(PEER_TOPK)) + 0.5


def _retrieve_kernel(x1_ref, g_ref, wq_ref, kblk_ref, pos_ref,
                     n2_ref, lena_ref, ca_ref, rankb_ref, pb_ref,
                     qt_ref, st_ref, v_ref, mark1_ref, *, tm):
    x1 = x1_ref[...]
    n2 = ((x1 * _rms_scale(x1)) * g_ref[...]).astype(BF16)
    n2_ref[...] = n2
    qt_ref[...] = _dot(wq_ref[...], n2, ((1,), (1,)))

    key_iota = lax.broadcasted_iota(jnp.int32, (1, PEER_NKEYS, LANES), 1).astype(F32)
    pos = jnp.broadcast_to(pos_ref[...], (pos_ref.shape[0], LANES))[None]

    def head_group(hp, _):
        heads = [HEADS_PER_TRIP * hp + k for k in range(HEADS_PER_TRIP)]
        for h in heads:
            q = qt_ref[pl.ds(pl.multiple_of(h * PEER_DK, PEER_DK), PEER_DK), :]
            st_ref[h] = _dot3(kblk_ref[h], q).reshape(2, PEER_NKEYS, tm)
        for lb in range(tm // LANES):
            lanes = slice(lb * LANES, (lb + 1) * LANES)
            s = jnp.concatenate([st_ref[h, :, :, lanes] for h in heads], axis=0)
            marked, vals = _mark_top16(s, key_iota)
            for k, h in enumerate(heads):
                v_ref[h, :, :, lanes] = vals[2 * k:2 * k + 2]
                mark1_ref[h, :, lanes] = marked[2 * k]
                rankb_ref[h, :, lanes] = _rank_plus_half(marked[2 * k + 1]).astype(BF16)
                pb_ref[h, :, lanes] = jnp.exp(s[2 * k + 1] - vals[2 * k + 1, 0:1]).astype(BF16)
        return 0

    lax.fori_loop(0, PEER_HEADS // HEADS_PER_TRIP, head_group, 0)

    for lb in range(tm // LANES):
        lanes = slice(lb * LANES, (lb + 1) * LANES)
        v1 = v_ref[:, 0, :, lanes]
        v2 = v_ref[:, 1, :, lanes]
        pieces = []
        for kind, fixed, start, count, first_valid in _candidate_slabs():
            if kind == "row":
                piece = v1[:, fixed:fixed + 1] + v2[:, start:start + count]
            else:
                piece = v1[:, start:start + count] + v2[:, fixed:fixed + 1]
            if first_valid:
                local = lax.broadcasted_iota(jnp.int32, piece.shape, 1)
                piece = jnp.where(local >= first_valid, piece, NEG_INF)
            pieces.append(piece)
        cand = jnp.concatenate(pieces, axis=1)
        picked = _is_marked(_mark_top16(cand, pos)[0])
        t0 = v1[:, 0:1] + v2[:, 0:1]
        z = jnp.sum(jnp.where(picked, jnp.exp(cand - t0), 0.0), axis=1, keepdims=True)
        count_i = [None] * PEER_TOPK
        off = 0
        for kind, fixed, start, count, first_valid in _candidate_slabs():
            hit = jnp.where(picked[:, off:off + count], 1.0, 0.0)
            if kind == "row":
                add = [(fixed, jnp.sum(hit, axis=1, keepdims=True))]
            else:
                add = [(start + r, hit[:, r:r + 1]) for r in range(first_valid, count)]
            for i, term in add:
                count_i[i] = term if count_i[i] is None else count_i[i] + term
            off += count
        for h in range(PEER_HEADS):
            marked1 = mark1_ref[h, :, lanes]
            len_a = jnp.zeros((PEER_NKEYS, LANES), F32)
            for i in range(PEER_TOPK):
                len_a = jnp.where(marked1 == MARK_BASE - i * MARK_STEP, count_i[i][h], len_a)
            lena_ref[h, :, lanes] = len_a
            ca_ref[h, :, lanes] = jnp.exp(st_ref[h, 0, :, lanes] - v1[h, 0:1]) / (2.0 * z[h])


def _retrieve(x1, g2, wq_t, kblk, tm):
    n_tok = x1.shape[0]
    assert tm % LANES == 0, "selection runs on whole lane groups of tokens"
    pos = jnp.asarray(_candidate_positions())
    key_spec = pl.BlockSpec((PEER_HEADS, PEER_NKEYS, tm), lambda i: (0, 0, i))
    per_key = lambda dtype: jax.ShapeDtypeStruct((PEER_HEADS, PEER_NKEYS, n_tok), dtype)
    return pl.pallas_call(
        functools.partial(_retrieve_kernel, tm=tm),
        grid=(n_tok // tm,),
        in_specs=[
            pl.BlockSpec((tm, D_MODEL), lambda i: (i, 0)),
            pl.BlockSpec((1, D_MODEL), lambda i: (0, 0)),
            pl.BlockSpec((D_MODEL, D_MODEL), lambda i: (0, 0)),
            pl.BlockSpec((PEER_HEADS, 2 * PEER_NKEYS, PEER_DK), lambda i: (0, 0, 0)),
            pl.BlockSpec(pos.shape, lambda i: (0, 0)),
        ],
        out_specs=[pl.BlockSpec((tm, D_MODEL), lambda i: (i, 0)), key_spec, key_spec, key_spec, key_spec],
        out_shape=[jax.ShapeDtypeStruct((n_tok, D_MODEL), BF16), per_key(F32), per_key(F32),
                   per_key(BF16), per_key(BF16)],
        scratch_shapes=[
            pltpu.VMEM((D_MODEL, tm), F32),
            pltpu.VMEM((PEER_HEADS, 2, PEER_NKEYS, tm), F32),
            pltpu.VMEM((PEER_HEADS, 2, PEER_TOPK, tm), F32),
            pltpu.VMEM((PEER_HEADS, PEER_NKEYS, tm), F32),
        ],
        compiler_params=_params("arbitrary"),
        name="retrieve",
    )(x1, g2, wq_t, kblk, pos)


EXPERT_TILE = 1024
A_PER_TILE = EXPERT_TILE // PEER_NKEYS


def _experts_kernel(x1_ref, n2_ref, u_ref, vt_ref, lena_ref, ca_ref, rankb_ref, pb_ref, fg_ref,
                    y_ref, acc_ref, *, tm):
    j = pl.program_id(1)

    @pl.when(j == 0)
    def _():
        acc_ref[...] = jnp.zeros_like(acc_ref)

    s = _dot(u_ref[...], n2_ref[...], ((1,), (1,)))
    pack = 2 * SUBLANES
    groups = PEER_NKEYS // pack
    acts = []
    for al in range(A_PER_TILE):
        w = None
        for h in range(PEER_HEADS):
            len_a = jnp.broadcast_to(lena_ref[h, al:al + 1, :], (pack, tm)).astype(BF16)
            c_a = jnp.broadcast_to(ca_ref[h, al:al + 1, :], (pack, tm)).astype(BF16)
            rank_b = rankb_ref[h].reshape(groups, pack, tm)
            p_b = pb_ref[h].reshape(groups, pack, tm)
            t = jnp.where(rank_b < len_a[None], p_b, jnp.zeros_like(p_b)) * c_a[None]
            w = t if w is None else w + t
        sa = s[al * PEER_NKEYS:(al + 1) * PEER_NKEYS]
        act = sa * (1.0 + lax.erf(sa * (1.0 / math.sqrt(2.0))))
        acts.append(act.astype(BF16) * w.reshape(PEER_NKEYS, tm))
    acc_ref[...] += _dot(vt_ref[...], jnp.concatenate(acts, axis=0))

    @pl.when(j == pl.num_programs(1) - 1)
    def _():
        x2 = x1_ref[...] + acc_ref[...].T
        y_ref[...] = (x2 * _rms_scale(x2)) * fg_ref[...]


def _experts(x1, n2, u_bf16, vt_bf16, len_a, c_a, rank_b, p_b, final_g, tm):
    n_tok = x1.shape[0]
    n_exp = u_bf16.shape[0]
    a_spec = pl.BlockSpec((PEER_HEADS, A_PER_TILE, tm), lambda i, j: (0, j, i))
    b_spec = pl.BlockSpec((PEER_HEADS, PEER_NKEYS, tm), lambda i, j: (0, 0, i))
    return pl.pallas_call(
        functools.partial(_experts_kernel, tm=tm),
        grid=(n_tok // tm, n_exp // EXPERT_TILE),
        in_specs=[
            pl.BlockSpec((tm, D_MODEL), lambda i, j: (i, 0)),
            pl.BlockSpec((tm, D_MODEL), lambda i, j: (i, 0)),
            pl.BlockSpec((EXPERT_TILE, D_MODEL), lambda i, j: (j, 0)),
            pl.BlockSpec((D_MODEL, EXPERT_TILE), lambda i, j: (0, j)),
            a_spec, a_spec, b_spec, b_spec,
            pl.BlockSpec((1, D_MODEL), lambda i, j: (0, 0)),
        ],
        out_specs=pl.BlockSpec((tm, D_MODEL), lambda i, j: (i, 0)),
        out_shape=jax.ShapeDtypeStruct((n_tok, D_MODEL), F32),
        scratch_shapes=[pltpu.VMEM((D_MODEL, tm), F32)],
        compiler_params=_params("arbitrary", "arbitrary"),
        name="experts",
    )(x1, n2, u_bf16, vt_bf16, len_a, c_a, rank_b, p_b, final_g)


def _token_tile(n_tok, cap):
    tm = cap
    while n_tok % tm:
        tm //= 2
    return tm


def _stream(x, s0, dn_buf0, h0, lru_buf0, p):
    b, t, _ = x.shape
    n_tok = b * t
    rows = min(CHUNK, t)
    xf = x.reshape(n_tok, D_MODEL)
    proj = _in_proj(xf, p["ln1_g"], p["w_in"], _token_tile(n_tok, 1024))
    proj3 = proj.reshape(b, t, PROJ_COLS)
    o_a, s_new, dn_buf = _deltanet(proj3, s0, dn_buf0, p["dn_conv_w"], p["alog_row"], p["dtb_row"],
                                   p["dn_norm_g"], rows, min(DN_CHUNKS_PER_STEP, t // rows))
    lru_rows = _token_tile(t, 256)
    o_b, h_new, lru_buf = _rglru(proj3, h0.reshape(b, 1, LRU_WIDTH), lru_buf0, p["lru_conv_w"],
                                 p["lru_conv_b"], p["lru_w_r"], p["lru_b_r"], p["lru_w_i"],
                                 p["lru_b_i"], p["lru_lambda"], lru_rows)
    x1 = _merge(xf, o_a.reshape(n_tok, DN_V), o_b.reshape(n_tok, LRU_WIDTH), proj,
                p["w_branch_a"], p["w_branch_b"], p["w_out"], _token_tile(n_tok, 512))
    tr = _token_tile(n_tok, 256)
    n2, len_a, c_a, rank_b, p_b = _retrieve(x1, p["ln2_g"], p["wq_t"], p["kblk"], tr)
    y = _experts(x1, n2, p["peer_u"], p["peer_vt"], len_a, c_a, rank_b, p_b, p["final_g"],
                 _token_tile(n_tok, 1024))
    return (y.reshape(b, t, D_MODEL), s_new, dn_buf, h_new.reshape(b, LRU_WIDTH), lru_buf)


def _prepare(w_in, dn_conv_w, dn_a_log, dn_dt_bias, dn_norm_g, lru_conv_w, lru_conv_b, lru_w_r,
             lru_b_r, lru_w_i, lru_b_i, lru_lambda, w_branch_a, w_branch_b, w_out, ln1_g, ln2_g,
             peer_w_q, peer_keys, peer_u, peer_v, final_g):
    row = lambda v: v.reshape(1, -1).astype(F32)
    lane_row = lambda v: jnp.zeros((1, LANES), F32).at[0, DN_HEADS:2 * DN_HEADS].set(v)
    w_in_p = jnp.pad(w_in, ((0, 0), (0, PROJ_COLS - w_in.shape[1]))).astype(BF16)
    wq_t = peer_w_q.T.astype(BF16)
    zeros = jnp.zeros((PEER_HEADS, PEER_NKEYS, PEER_DKH), F32)
    kblk = jnp.concatenate([
        jnp.concatenate([peer_keys[:, 0], zeros], axis=-1),
        jnp.concatenate([zeros, peer_keys[:, 1]], axis=-1)], axis=1)
    return dict(
        w_in=w_in_p, dn_conv_w=dn_conv_w, alog_row=lane_row(dn_a_log), dtb_row=lane_row(dn_dt_bias),
        dn_norm_g=row(dn_norm_g), lru_conv_w=lru_conv_w, lru_conv_b=row(lru_conv_b),
        lru_w_r=lru_w_r.astype(BF16), lru_b_r=row(lru_b_r), lru_w_i=lru_w_i.astype(BF16),
        lru_b_i=row(lru_b_i), lru_lambda=row(lru_lambda), w_branch_a=w_branch_a.astype(BF16),
        w_branch_b=w_branch_b.astype(BF16), w_out=w_out.astype(BF16), ln1_g=row(ln1_g),
        ln2_g=row(ln2_g), wq_t=wq_t, kblk=kblk, peer_u=peer_u.astype(BF16),
        peer_vt=peer_v.T.astype(BF16), final_g=row(final_g))


def kernel(x_prompt, x_sample, state_dn, state_dn_conv, state_lru_h, state_lru_conv, w_in, dn_conv_w, dn_a_log, dn_dt_bias, dn_norm_g, lru_conv_w, lru_conv_b, lru_w_r, lru_b_r, lru_w_i, lru_b_i, lru_lambda, w_branch_a, w_branch_b, w_out, ln1_g, ln2_g, peer_w_q, peer_keys, peer_u, peer_v, final_g):
    depth = w_in.shape[0]
    assert depth == 1, "the final norm is fused into the layer's last kernel"
    bp = x_prompt.shape[0]
    dt = x_prompt.dtype
    p = _prepare(w_in[0], dn_conv_w[0], dn_a_log[0], dn_dt_bias[0], dn_norm_g[0], lru_conv_w[0],
                 lru_conv_b[0], lru_w_r[0], lru_b_r[0], lru_w_i[0], lru_b_i[0], lru_lambda[0],
                 w_branch_a[0], w_branch_b[0], w_out[0], ln1_g[0], ln2_g[0], peer_w_q[0],
                 peer_keys[0], peer_u[0], peer_v[0], final_g)
    prompt = _stream(
        x_prompt,
        jnp.zeros((bp, DN_HEADS, DN_DK, DN_DV), dt),
        jnp.zeros((bp, CONV_W - 1, DN_CONV_CH), dt),
        jnp.zeros((bp, LRU_WIDTH), dt),
        jnp.zeros((bp, CONV_W - 1, LRU_WIDTH), dt),
        p)
    sample = _stream(x_sample, state_dn[0], state_dn_conv[0], state_lru_h[0], state_lru_conv[0], p)
    y_p, s_p, db_p, h_p, lb_p = prompt
    y_s, s_s, db_s, h_s, lb_s = sample
    return (y_p, y_s, s_p[None], db_p[None], h_p[None], lb_p[None],
            s_s[None], db_s[None], h_s[None], lb_s[None])
```

```python
import functools
import math

import jax
import jax.numpy as jnp
import numpy as np
from jax import lax
from jax.experimental import pallas as pl
from jax.experimental.pallas import tpu as pltpu

F32 = jnp.float32
BF16 = jnp.bfloat16

D_MODEL = 1024
CONV_W = 4
EPS = 1e-6
DN_HEADS = 8
DN_DK = 128
DN_DV = 128
DN_QK = DN_HEADS * DN_DK
DN_V = DN_HEADS * DN_DV
DN_CONV_CH = 2 * DN_QK + DN_V
LRU_WIDTH = D_MODEL
LRU_BLOCKS = 8
LRU_BDIM = LRU_WIDTH // LRU_BLOCKS
LRU_C = 8.0
PEER_HEADS = 8
PEER_NKEYS = 128
PEER_DK = 128
PEER_DKH = PEER_DK // 2
PEER_TOPK = 16
CHUNK = 64

LANES = 128
SUBLANES = 8
MAIN_COLS = 8192
PROJ_COLS = MAIN_COLS + LANES
PROJ_COL_TILE = PROJ_COLS // 5
BD_BLOCK = MAIN_COLS // LANES
TAIL_ROW = SUBLANES - (CONV_W - 1)
VMEM_LIMIT = 56 * 1024 * 1024
NEG_INF = float("-inf")


def _params(*sem):
    return pltpu.CompilerParams(dimension_semantics=sem, vmem_limit_bytes=VMEM_LIMIT)


def _split3(x):
    a = x.astype(BF16)
    r = x - a.astype(F32)
    b = r.astype(BF16)
    c = (r - b.astype(F32)).astype(BF16)
    return a, b, c


def _dot(a, b, dims=None):
    if dims is None:
        return jnp.dot(a, b, preferred_element_type=F32)
    return lax.dot_general(a, b, (dims, ((), ())), preferred_element_type=F32)


def _dot3(a, b, dims=None):
    ah = a.astype(BF16)
    al = (a - ah.astype(F32)).astype(BF16)
    bh = b.astype(BF16)
    bl = (b - bh.astype(F32)).astype(BF16)
    return _dot(ah, bh, dims) + (_dot(ah, bl, dims) + _dot(al, bh, dims))


def _dot_exact_lhs(a_bf16, b, dims=None):
    b0, b1, b2 = _split3(b)
    return _dot(a_bf16, b0, dims) + (_dot(a_bf16, b1, dims) + _dot(a_bf16, b2, dims))


def _sigmoid(x):
    return 1.0 / (1.0 + jnp.exp(-x))


def _softplus(x):
    return jnp.maximum(x, 0.0) + jnp.log1p(jnp.exp(-jnp.abs(x)))


def _rms_scale(x):
    return lax.rsqrt(jnp.mean(x * x, axis=-1, keepdims=True) + EPS)


def _causal_conv(xpad_ref, new_rows, w, rows):
    xpad_ref[SUBLANES:SUBLANES + rows, :] = new_rows
    xa = xpad_ref[...]
    y = None
    for j in range(CONV_W):
        delay = CONV_W - 1 - j
        shifted = pltpu.roll(xa, delay, 0) if delay else xa
        term = shifted[SUBLANES:SUBLANES + rows, :] * w[j:j + 1, :]
        y = term if y is None else y + term
    tail = xpad_ref[rows + TAIL_ROW:rows + SUBLANES, :]
    xpad_ref[TAIL_ROW:SUBLANES, :] = tail
    return y, tail


def _inproj_kernel(x_ref, g_ref, w_ref, o_ref):
    x = x_ref[...]
    n = (x * _rms_scale(x)) * g_ref[...]
    o_ref[...] = _dot(n.astype(BF16), w_ref[...])


def _in_proj(x, g, w_bf16, tm):
    n_tok = x.shape[0]
    return pl.pallas_call(
        _inproj_kernel,
        grid=(PROJ_COLS // PROJ_COL_TILE, n_tok // tm),
        in_specs=[
            pl.BlockSpec((tm, D_MODEL), lambda j, i: (i, 0)),
            pl.BlockSpec((1, D_MODEL), lambda j, i: (0, 0)),
            pl.BlockSpec((D_MODEL, PROJ_COL_TILE), lambda j, i: (0, j)),
        ],
        out_specs=pl.BlockSpec((tm, PROJ_COL_TILE), lambda j, i: (i, j)),
        out_shape=jax.ShapeDtypeStruct((n_tok, PROJ_COLS), F32),
        compiler_params=_params("arbitrary", "arbitrary"),
        name="in_proj",
    )(x, g, w_bf16)


def _bdot(a, b, eq):
    return jnp.einsum(eq, a, b, preferred_element_type=F32)


def _bdot1(a, b, eq):
    return _bdot(a.astype(BF16), b.astype(BF16), eq)


def _split2(a):
    hi = a.astype(BF16)
    return hi, (a - hi.astype(F32)).astype(BF16)


def _bdot3p(a_parts, b_parts, eq):
    ah, al = a_parts
    bh, bl = b_parts
    return _bdot(ah, bh, eq) + (_bdot(ah, bl, eq) + _bdot(al, bh, eq))


MM = "gij,gjk->gik"
DN_CHUNKS_PER_STEP = 4


def _unit_lower_inverse(nmat, rows, row_i, col_i):
    eye = jnp.where(row_i == col_i, 1.0, 0.0)
    blk = SUBLANES
    differ = row_i ^ col_i
    n_parts = _split2(nmat)
    zero = jnp.zeros_like(n_parts[0])

    def block_parts(mask):
        return tuple(jnp.where(mask, part, zero) for part in n_parts)

    nd_parts = block_parts(differ < blk)
    t = eye - jnp.where(differ < blk, nmat, 0.0)
    m_parts = _split2(_bdot3p(nd_parts, nd_parts, MM))
    t = t + _bdot3p(_split2(t), m_parts, MM)
    m_parts = _split2(_bdot3p(m_parts, m_parts, MM))
    t = t + _bdot3p(_split2(t), m_parts, MM)
    while blk < rows:
        off_parts = block_parts((differ >> int(math.log2(blk))) == 1)
        t_parts = _split2(t)
        t = t - _bdot3p(t_parts, _split2(_bdot3p(off_parts, t_parts, MM)), MM)
        blk *= 2
    return t


def _deltanet_kernel(qkv_ref, z_ref, bd_ref, s0_ref, buf0_ref, convw_ref, alog_ref, dtb_ref,
                     ng_ref, o_ref, s_ref, buf_ref, xpad_ref, *, rows, chunks):
    span = rows * chunks

    @pl.when(pl.program_id(1) == 0)
    def _():
        xpad_ref[TAIL_ROW:SUBLANES, :] = buf0_ref[0]
        s_ref[...] = s0_ref[...]

    y, tail = _causal_conv(xpad_ref, qkv_ref[0], convw_ref[...], span)
    buf_ref[0] = tail
    act = y * _sigmoid(y)

    bd = bd_ref[0]
    beta_all = _sigmoid(bd)
    g_all = -jnp.exp(alog_ref[...]) * _softplus(bd + dtb_ref[...])
    t_row = lax.broadcasted_iota(jnp.int32, (span, span), 0)
    t_col = lax.broadcasted_iota(jnp.int32, (span, span), 1)
    tri = jnp.where((t_row ^ t_col) < rows, jnp.where(t_row >= t_col, 1.0, 0.0), 0.0).astype(BF16)
    gcum = _dot_exact_lhs(tri, g_all)
    pad = -span % LANES
    gcum_t = (jnp.concatenate([gcum, jnp.zeros((pad, LANES), F32)], axis=0) if pad else gcum).T

    def per_head(src, col0, width):
        return jnp.stack([src[c * rows:(c + 1) * rows, col0 + h * width:col0 + (h + 1) * width]
                          for c in range(chunks) for h in range(DN_HEADS)])

    q = per_head(act, 0, DN_DK)
    k = per_head(act, DN_QK, DN_DK)
    v = per_head(act, 2 * DN_QK, DN_DV)
    z = per_head(z_ref[0], 0, DN_DV)
    beta = per_head(beta_all, 0, 1)
    gc = per_head(gcum, DN_HEADS, 1)
    gr = jnp.stack([gcum_t[DN_HEADS + h:DN_HEADS + h + 1, c * rows:(c + 1) * rows]
                    for c in range(chunks) for h in range(DN_HEADS)])
    g_last = gc[:, rows - 1:rows, :]

    q = q * (lax.rsqrt(jnp.sum(q * q, axis=-1, keepdims=True) + EPS) * (DN_DK ** -0.5))
    k = k * lax.rsqrt(jnp.sum(k * k, axis=-1, keepdims=True) + EPS)
    row_i = lax.broadcasted_iota(jnp.int32, (rows, rows), 0)
    col_i = lax.broadcasted_iota(jnp.int32, (rows, rows), 1)
    incl = row_i >= col_i
    dec = jnp.where(incl, jnp.exp(jnp.where(incl, gc - gr, 0.0)), 0.0)
    kb = k.astype(BF16)
    kk = _bdot(kb, kb, "gik,gjk->gij")
    nmat = jnp.where(row_i > col_i, beta * kk * dec, 0.0)
    gam = jnp.exp(gc)
    rhs = jnp.concatenate([(beta * gam) * k, beta * v], axis=-1)
    sol = _bdot1(_unit_lower_inverse(nmat, rows, row_i, col_i), rhs, MM)
    uv = sol[:, :, DN_DK:]
    wq = jnp.concatenate([sol[:, :, :DN_DK], q * gam], axis=1)
    qk = _bdot(q.astype(BF16), kb, "gik,gjk->gij") * dec
    kend = k * jnp.exp(g_last - gc)
    gend = jnp.exp(g_last)
    gate = ng_ref[...] * (z * _sigmoid(z))

    for c in range(chunks):
        hs = slice(c * DN_HEADS, (c + 1) * DN_HEADS)
        s_old = s_ref[0]
        ws = _bdot1(wq[hs], s_old, MM)
        u = uv[hs] - ws[:, :rows]
        ub = u.astype(BF16)
        o = ws[:, rows:] + _bdot(qk[hs].astype(BF16), ub, MM)
        s_ref[0] = gend[hs] * s_old + _bdot(kend[hs].astype(BF16), ub, "glk,glv->gkv")
        o = (o * _rms_scale(o)) * gate[hs]
        for h in range(DN_HEADS):
            o_ref[0, c * rows:(c + 1) * rows, h * DN_DV:(h + 1) * DN_DV] = o[h]


def _deltanet(proj3, s0, buf0, convw, alog_row, dtb_row, norm_g, rows, chunks):
    b, t, _ = proj3.shape
    span = rows * chunks
    nc = t // span
    rows, rows_per_chunk = span, rows
    return pl.pallas_call(
        functools.partial(_deltanet_kernel, rows=rows_per_chunk, chunks=chunks),
        grid=(b, nc),
        in_specs=[
            pl.BlockSpec((1, rows, DN_CONV_CH), lambda i, c: (i, c, 0)),
            pl.BlockSpec((1, rows, DN_V), lambda i, c: (i, c, DN_CONV_CH // DN_V)),
            pl.BlockSpec((1, rows, LANES), lambda i, c: (i, c, BD_BLOCK)),
            pl.BlockSpec((1, DN_HEADS, DN_DK, DN_DV), lambda i, c: (i, 0, 0, 0)),
            pl.BlockSpec((1, CONV_W - 1, DN_CONV_CH), lambda i, c: (i, 0, 0)),
            pl.BlockSpec((CONV_W, DN_CONV_CH), lambda i, c: (0, 0)),
            pl.BlockSpec((1, LANES), lambda i, c: (0, 0)),
            pl.BlockSpec((1, LANES), lambda i, c: (0, 0)),
            pl.BlockSpec((1, DN_DV), lambda i, c: (0, 0)),
        ],
        out_specs=[
            pl.BlockSpec((1, rows, DN_V), lambda i, c: (i, c, 0)),
            pl.BlockSpec((1, DN_HEADS, DN_DK, DN_DV), lambda i, c: (i, 0, 0, 0)),
            pl.BlockSpec((1, CONV_W - 1, DN_CONV_CH), lambda i, c: (i, 0, 0)),
        ],
        out_shape=[
            jax.ShapeDtypeStruct((b, t, DN_V), F32),
            jax.ShapeDtypeStruct((b, DN_HEADS, DN_DK, DN_DV), F32),
            jax.ShapeDtypeStruct((b, CONV_W - 1, DN_CONV_CH), F32),
        ],
        scratch_shapes=[pltpu.VMEM((SUBLANES + rows, DN_CONV_CH), F32)],
        compiler_params=_params("arbitrary", "arbitrary"),
        name="deltanet",
    )(proj3, proj3, proj3, s0, buf0, convw, alog_row, dtb_row, norm_g)


def _rglru_kernel(lx_ref, ly_ref, h0_ref, buf0_ref, convw_ref, convb_ref, wr_ref, br_ref,
                  wi_ref, bi_ref, lam_ref, o_ref, h_ref, buf_ref, xpad_ref, a_ref, b_ref, *, rows):
    c = pl.program_id(1)

    @pl.when(c == 0)
    def _():
        xpad_ref[TAIL_ROW:SUBLANES, :] = buf0_ref[0]
        h_ref[...] = h0_ref[...]

    y, tail = _causal_conv(xpad_ref, lx_ref[0], convw_ref[...], rows)
    buf_ref[0] = tail
    x = y + convb_ref[...]
    xb = x.astype(BF16)
    sp = LRU_C * _softplus(-lam_ref[...])
    for n in range(LRU_BLOCKS):
        lo = n * LRU_BDIM
        xn = xb[:, lo:lo + LRU_BDIM]
        r = _sigmoid(_dot(xn, wr_ref[n]) + br_ref[:, lo:lo + LRU_BDIM])
        i = _sigmoid(_dot(xn, wi_ref[n]) + bi_ref[:, lo:lo + LRU_BDIM])
        log_a = -(r * sp[:, lo:lo + LRU_BDIM])
        a = jnp.exp(log_a)
        a_ref[:, lo:lo + LRU_BDIM] = a
        one_minus_a2 = -jnp.tanh(log_a) * (a * a + 1.0)
        b_ref[:, lo:lo + LRU_BDIM] = jnp.sqrt(one_minus_a2) * (i * x[:, lo:lo + LRU_BDIM])

    groups = rows // SUBLANES
    a = a_ref[...].reshape(groups, SUBLANES, LRU_WIDTH)
    b = b_ref[...].reshape(groups, SUBLANES, LRU_WIDTH)
    sub = lax.broadcasted_iota(jnp.int32, (groups, SUBLANES, LRU_WIDTH), 1)
    d = 1
    while d < SUBLANES:
        keep = sub >= d
        a_prev = jnp.where(keep, pltpu.roll(a, d, 1), 1.0)
        b_prev = jnp.where(keep, pltpu.roll(b, d, 1), 0.0)
        b = a * b_prev + b
        a = a * a_prev
        d *= 2
    a_ref[...] = a.reshape(rows, LRU_WIDTH)
    b_ref[...] = b.reshape(rows, LRU_WIDTH)

    def group_step(g, h):
        r0 = pl.multiple_of(g * SUBLANES, SUBLANES)
        hs = a_ref[pl.ds(r0, SUBLANES), :] * h + b_ref[pl.ds(r0, SUBLANES), :]
        o_ref[0, pl.ds(r0, SUBLANES), :] = hs
        return hs[SUBLANES - 1:SUBLANES, :]

    h_ref[0] = lax.fori_loop(0, groups, group_step, h_ref[0], unroll=4)
    o_ref[0] = o_ref[0] * jax.nn.gelu(ly_ref[0], approximate=True)


def _rglru(proj3, h0, buf0, convw, convb, w_r, b_r, w_i, b_i, lam, rows):
    b, t, _ = proj3.shape
    nc = t // rows
    row = lambda i, c: (0, 0)
    return pl.pallas_call(
        functools.partial(_rglru_kernel, rows=rows),
        grid=(b, nc),
        in_specs=[
            pl.BlockSpec((1, rows, LRU_WIDTH), lambda i, c: (i, c, 4)),
            pl.BlockSpec((1, rows, LRU_WIDTH), lambda i, c: (i, c, 5)),
            pl.BlockSpec((1, 1, LRU_WIDTH), lambda i, c: (i, 0, 0)),
            pl.BlockSpec((1, CONV_W - 1, LRU_WIDTH), lambda i, c: (i, 0, 0)),
            pl.BlockSpec((CONV_W, LRU_WIDTH), row),
            pl.BlockSpec((1, LRU_WIDTH), row),
            pl.BlockSpec((LRU_BLOCKS, LRU_BDIM, LRU_BDIM), lambda i, c: (0, 0, 0)),
            pl.BlockSpec((1, LRU_WIDTH), row),
            pl.BlockSpec((LRU_BLOCKS, LRU_BDIM, LRU_BDIM), lambda i, c: (0, 0, 0)),
            pl.BlockSpec((1, LRU_WIDTH), row),
            pl.BlockSpec((1, LRU_WIDTH), row),
        ],
        out_specs=[
            pl.BlockSpec((1, rows, LRU_WIDTH), lambda i, c: (i, c, 0)),
            pl.BlockSpec((1, 1, LRU_WIDTH), lambda i, c: (i, 0, 0)),
            pl.BlockSpec((1, CONV_W - 1, LRU_WIDTH), lambda i, c: (i, 0, 0)),
        ],
        out_shape=[
            jax.ShapeDtypeStruct((b, t, LRU_WIDTH), F32),
            jax.ShapeDtypeStruct((b, 1, LRU_WIDTH), F32),
            jax.ShapeDtypeStruct((b, CONV_W - 1, LRU_WIDTH), F32),
        ],
        scratch_shapes=[
            pltpu.VMEM((SUBLANES + rows, LRU_WIDTH), F32),
            pltpu.VMEM((rows, LRU_WIDTH), F32),
            pltpu.VMEM((rows, LRU_WIDTH), F32),
        ],
        compiler_params=_params("arbitrary", "arbitrary"),
        name="rglru",
    )(proj3, proj3, h0, buf0, convw, convb, w_r, b_r, w_i, b_i, lam)


def _merge_kernel(x_ref, oa_ref, ob_ref, ga_ref, gb_ref, wa_ref, wb_ref, wo_ref, x1_ref):
    ya = _dot(oa_ref[...].astype(BF16), wa_ref[...])
    yb = _dot(ob_ref[...].astype(BF16), wb_ref[...])
    mix = _sigmoid(ga_ref[...]) * ya + _sigmoid(gb_ref[...]) * yb
    x1_ref[...] = x_ref[...] + _dot(mix.astype(BF16), wo_ref[...])


def _merge(x, o_a, o_b, proj, w_a, w_b, w_o, tm):
    n_tok = x.shape[0]
    tok = lambda i: (i, 0)
    full = lambda i: (0, 0)
    return pl.pallas_call(
        _merge_kernel,
        grid=(n_tok // tm,),
        in_specs=[
            pl.BlockSpec((tm, D_MODEL), tok),
            pl.BlockSpec((tm, DN_V), tok),
            pl.BlockSpec((tm, LRU_WIDTH), tok),
            pl.BlockSpec((tm, D_MODEL), lambda i: (i, 6)),
            pl.BlockSpec((tm, D_MODEL), lambda i: (i, 7)),
            pl.BlockSpec((DN_V, D_MODEL), full),
            pl.BlockSpec((LRU_WIDTH, D_MODEL), full),
            pl.BlockSpec((D_MODEL, D_MODEL), full),
        ],
        out_specs=pl.BlockSpec((tm, D_MODEL), tok),
        out_shape=jax.ShapeDtypeStruct((n_tok, D_MODEL), F32),
        compiler_params=_params("arbitrary"),
        name="merge",
    )(x, o_a, o_b, proj, proj, w_a, w_b, w_o)


def _candidate_slabs():
    return (
        ("row", 0, 0, 16, 0), ("row", 1, 0, 8, 0), ("row", 2, 0, 8, 0), ("row", 3, 0, 8, 0),
        ("col", 0, 8, 8, 0), ("col", 0, 0, 8, 4), ("col", 1, 0, 8, 4), ("col", 2, 0, 8, 4),
    )


def _candidate_positions():
    pos = []
    for kind, fixed, start, count, first_valid in _candidate_slabs():
        for r in range(start, start + count):
            i, j = (fixed, r) if kind == "row" else (r, fixed)
            masked = r - start < first_valid
            pos.append(PEER_TOPK * PEER_TOPK + len(pos) if masked else i * PEER_TOPK + j)
    assert len(set(pos)) == len(pos)
    return np.asarray(pos, np.float32).reshape(-1, 1)


MARK_BASE = -(2.0 ** 127 + 2.0 ** 126)
MARK_STEP = 2.0 ** 120
ORDER_BIG = float(1 << 20)
HEADS_PER_TRIP = 4


def _mark_top16(s, order):
    vals = []
    for r in range(PEER_TOPK):
        m = jnp.max(s, axis=1, keepdims=True)
        first = jnp.min(jnp.where(s == m, order, ORDER_BIG), axis=1, keepdims=True)
        s = jnp.where(order == first, MARK_BASE - r * MARK_STEP, s)
        vals.append(m)
    return s, jnp.concatenate(vals, axis=1)


def _is_marked(x):
    return (x <= MARK_BASE) & (x > MARK_BASE - PEER_TOPK * MARK_STEP)


def _rank_plus_half(marked):
    return jnp.where(_is_marked(marked), (MARK_BASE - marked) * (1.0 / MARK_STEP), float(PEER_TOPK)) + 0.5


def _retrieve_kernel(x1_ref, g_ref, wq_ref, kblk_ref, pos_ref,
                     n2_ref, lena_ref, ca_ref, rankb_ref, pb_ref,
                     qt_ref, st_ref, v_ref, mark1_ref, *, tm):
    x1 = x1_ref[...]
    n2 = ((x1 * _rms_scale(x1)) * g_ref[...]).astype(BF16)
    n2_ref[...] = n2
    qt_ref[...] = _dot(wq_ref[...], n2, ((1,), (1,)))

    key_iota = lax.broadcasted_iota(jnp.int32, (1, PEER_NKEYS, LANES), 1).astype(F32)
    pos = jnp.broadcast_to(pos_ref[...], (pos_ref.shape[0], LANES))[None]

    def head_group(hp, _):
        heads = [HEADS_PER_TRIP * hp + k for k in range(HEADS_PER_TRIP)]
        for h in heads:
            q = qt_ref[pl.ds(pl.multiple_of(h * PEER_DK, PEER_DK), PEER_DK), :]
            st_ref[h] = _dot3(kblk_ref[h], q).reshape(2, PEER_NKEYS, tm)
        for lb in range(tm // LANES):
            lanes = slice(lb * LANES, (lb + 1) * LANES)
            s = jnp.concatenate([st_ref[h, :, :, lanes] for h in heads], axis=0)
            marked, vals = _mark_top16(s, key_iota)
            for k, h in enumerate(heads):
                v_ref[h, :, :, lanes] = vals[2 * k:2 * k + 2]
                mark1_ref[h, :, lanes] = marked[2 * k]
                rankb_ref[h, :, lanes] = _rank_plus_half(marked[2 * k + 1]).astype(BF16)
                pb_ref[h, :, lanes] = jnp.exp(s[2 * k + 1] - vals[2 * k + 1, 0:1]).astype(BF16)
        return 0

    lax.fori_loop(0, PEER_HEADS // HEADS_PER_TRIP, head_group, 0)

    for lb in range(tm // LANES):
        lanes = slice(lb * LANES, (lb + 1) * LANES)
        v1 = v_ref[:, 0, :, lanes]
        v2 = v_ref[:, 1, :, lanes]
        pieces = []
        for kind, fixed, start, count, first_valid in _candidate_slabs():
            if kind == "row":
                piece = v1[:, fixed:fixed + 1] + v2[:, start:start + count]
            else:
                piece = v1[:, start:start + count] + v2[:, fixed:fixed + 1]
            if first_valid:
                local = lax.broadcasted_iota(jnp.int32, piece.shape, 1)
                piece = jnp.where(local >= first_valid, piece, NEG_INF)
            pieces.append(piece)
        cand = jnp.concatenate(pieces, axis=1)
        picked = _is_marked(_mark_top16(cand, pos)[0])
        t0 = v1[:, 0:1] + v2[:, 0:1]
        z = jnp.sum(jnp.where(picked, jnp.exp(cand - t0), 0.0), axis=1, keepdims=True)
        count_i = [None] * PEER_TOPK
        off = 0
        for kind, fixed, start, count, first_valid in _candidate_slabs():
            hit = jnp.where(picked[:, off:off + count], 1.0, 0.0)
            if kind == "row":
                add = [(fixed, jnp.sum(hit, axis=1, keepdims=True))]
            else:
                add = [(start + r, hit[:, r:r + 1]) for r in range(first_valid, count)]
            for i, term in add:
                count_i[i] = term if count_i[i] is None else count_i[i] + term
            off += count
        for h in range(PEER_HEADS):
            marked1 = mark1_ref[h, :, lanes]
            len_a = jnp.zeros((PEER_NKEYS, LANES), F32)
            for i in range(PEER_TOPK):
                len_a = jnp.where(marked1 == MARK_BASE - i * MARK_STEP, count_i[i][h], len_a)
            lena_ref[h, :, lanes] = len_a
            ca_ref[h, :, lanes] = jnp.exp(st_ref[h, 0, :, lanes] - v1[h, 0:1]) / (2.0 * z[h])


def _retrieve(x1, g2, wq_t, kblk, tm):
    n_tok = x1.shape[0]
    assert tm % LANES == 0, "selection runs on whole lane groups of tokens"
    pos = jnp.asarray(_candidate_positions())
    key_spec = pl.BlockSpec((PEER_HEADS, PEER_NKEYS, tm), lambda i: (0, 0, i))
    per_key = lambda dtype: jax.ShapeDtypeStruct((PEER_HEADS, PEER_NKEYS, n_tok), dtype)
    return pl.pallas_call(
        functools.partial(_retrieve_kernel, tm=tm),
        grid=(n_tok // tm,),
        in_specs=[
            pl.BlockSpec((tm, D_MODEL), lambda i: (i, 0)),
            pl.BlockSpec((1, D_MODEL), lambda i: (0, 0)),
            pl.BlockSpec((D_MODEL, D_MODEL), lambda i: (0, 0)),
            pl.BlockSpec((PEER_HEADS, 2 * PEER_NKEYS, PEER_DK), lambda i: (0, 0, 0)),
            pl.BlockSpec(pos.shape, lambda i: (0, 0)),
        ],
        out_specs=[pl.BlockSpec((tm, D_MODEL), lambda i: (i, 0)), key_spec, key_spec, key_spec, key_spec],
        out_shape=[jax.ShapeDtypeStruct((n_tok, D_MODEL), BF16), per_key(F32), per_key(F32),
                   per_key(BF16), per_key(BF16)],
        scratch_shapes=[
            pltpu.VMEM((D_MODEL, tm), F32),
            pltpu.VMEM((PEER_HEADS, 2, PEER_NKEYS, tm), F32),
            pltpu.VMEM((PEER_HEADS, 2, PEER_TOPK, tm), F32),
            pltpu.VMEM((PEER_HEADS, PEER_NKEYS, tm), F32),
        ],
        compiler_params=_params("arbitrary"),
        name="retrieve",
    )(x1, g2, wq_t, kblk, pos)


EXPERT_TILE = 1024
A_PER_TILE = EXPERT_TILE // PEER_NKEYS


def _experts_kernel(x1_ref, n2_ref, u_ref, v_ref, lena_ref, ca_ref, rankb_ref, pb_ref, fg_ref,
                    y_ref, acc_ref, *, tm):
    j = pl.program_id(1)

    @pl.when(j == 0)
    def _():
        acc_ref[...] = jnp.zeros_like(acc_ref)

    s = _dot(u_ref[...], n2_ref[...], ((1,), (1,)))
    pack = 2 * SUBLANES
    groups = PEER_NKEYS // pack
    acts = []
    for al in range(A_PER_TILE):
        w = None
        for h in range(PEER_HEADS):
            len_a = jnp.broadcast_to(lena_ref[h, al:al + 1, :], (pack, tm)).astype(BF16)
            c_a = jnp.broadcast_to(ca_ref[h, al:al + 1, :], (pack, tm)).astype(BF16)
            rank_b = rankb_ref[h].reshape(groups, pack, tm)
            p_b = pb_ref[h].reshape(groups, pack, tm)
            t = jnp.where(rank_b < len_a[None], p_b, jnp.zeros_like(p_b)) * c_a[None]
            w = t if w is None else w + t
        sa = s[al * PEER_NKEYS:(al + 1) * PEER_NKEYS]
        act = sa * (1.0 + lax.erf(sa * (1.0 / math.sqrt(2.0))))
        acts.append(act.astype(BF16) * w.reshape(PEER_NKEYS, tm))
    acc_ref[...] += _dot(v_ref[...], jnp.concatenate(acts, axis=0), ((0,), (0,)))

    @pl.when(j == pl.num_programs(1) - 1)
    def _():
        x2 = x1_ref[...] + acc_ref[...].T
        y_ref[...] = (x2 * _rms_scale(x2)) * fg_ref[...]


def _experts(x1, n2, u_bf16, v_bf16, len_a, c_a, rank_b, p_b, final_g, tm):
    n_tok = x1.shape[0]
    n_exp = u_bf16.shape[0]
    a_spec = pl.BlockSpec((PEER_HEADS, A_PER_TILE, tm), lambda i, j: (0, j, i))
    b_spec = pl.BlockSpec((PEER_HEADS, PEER_NKEYS, tm), lambda i, j: (0, 0, i))
    return pl.pallas_call(
        functools.partial(_experts_kernel, tm=tm),
        grid=(n_tok // tm, n_exp // EXPERT_TILE),
        in_specs=[
            pl.BlockSpec((tm, D_MODEL), lambda i, j: (i, 0)),
            pl.BlockSpec((tm, D_MODEL), lambda i, j: (i, 0)),
            pl.BlockSpec((EXPERT_TILE, D_MODEL), lambda i, j: (j, 0)),
            pl.BlockSpec((EXPERT_TILE, D_MODEL), lambda i, j: (j, 0)),
            a_spec, a_spec, b_spec, b_spec,
            pl.BlockSpec((1, D_MODEL), lambda i, j: (0, 0)),
        ],
        out_specs=pl.BlockSpec((tm, D_MODEL), lambda i, j: (i, 0)),
        out_shape=jax.ShapeDtypeStruct((n_tok, D_MODEL), F32),
        scratch_shapes=[pltpu.VMEM((D_MODEL, tm), F32)],
        compiler_params=_params("arbitrary", "arbitrary"),
        name="experts",
    )(x1, n2, u_bf16, v_bf16, len_a, c_a, rank_b, p_b, final_g)


def _token_tile(n_tok, cap):
    tm = cap
    while n_tok % tm:
        tm //= 2
    return tm


def _stream(x, s0, dn_buf0, h0, lru_buf0, p):
    b, t, _ = x.shape
    n_tok = b * t
    rows = min(CHUNK, t)
    xf = x.reshape(n_tok, D_MODEL)
    proj = _in_proj(xf, p["ln1_g"], p["w_in"], _token_tile(n_tok, 1024))
    proj3 = proj.reshape(b, t, PROJ_COLS)
    o_a, s_new, dn_buf = _deltanet(proj3, s0, dn_buf0, p["dn_conv_w"], p["alog_row"], p["dtb_row"],
                                   p["dn_norm_g"], rows, min(DN_CHUNKS_PER_STEP, t // rows))
    lru_rows = _token_tile(t, 256)
    o_b, h_new, lru_buf = _rglru(proj3, h0.reshape(b, 1, LRU_WIDTH), lru_buf0, p["lru_conv_w"],
                                 p["lru_conv_b"], p["lru_w_r"], p["lru_b_r"], p["lru_w_i"],
                                 p["lru_b_i"], p["lru_lambda"], lru_rows)
    x1 = _merge(xf, o_a.reshape(n_tok, DN_V), o_b.reshape(n_tok, LRU_WIDTH), proj,
                p["w_branch_a"], p["w_branch_b"], p["w_out"], _token_tile(n_tok, 512))
    tr = _token_tile(n_tok, 256)
    n2, len_a, c_a, rank_b, p_b = _retrieve(x1, p["ln2_g"], p["wq_t"], p["kblk"], tr)
    y = _experts(x1, n2, p["peer_u"], p["peer_v"], len_a, c_a, rank_b, p_b, p["final_g"],
                 _token_tile(n_tok, 1024))
    return (y.reshape(b, t, D_MODEL), s_new, dn_buf, h_new.reshape(b, LRU_WIDTH), lru_buf)


def _prepare(w_in, dn_conv_w, dn_a_log, dn_dt_bias, dn_norm_g, lru_conv_w, lru_conv_b, lru_w_r,
             lru_b_r, lru_w_i, lru_b_i, lru_lambda, w_branch_a, w_branch_b, w_out, ln1_g, ln2_g,
             peer_w_q, peer_keys, peer_u, peer_v, final_g):
    row = lambda v: v.reshape(1, -1).astype(F32)
    lane_row = lambda v: jnp.zeros((1, LANES), F32).at[0, DN_HEADS:2 * DN_HEADS].set(v)
    w_in_p = jnp.pad(w_in, ((0, 0), (0, PROJ_COLS - w_in.shape[1]))).astype(BF16)
    wq_t = peer_w_q.T.astype(BF16)
    zeros = jnp.zeros((PEER_HEADS, PEER_NKEYS, PEER_DKH), F32)
    kblk = jnp.concatenate([
        jnp.concatenate([peer_keys[:, 0], zeros], axis=-1),
        jnp.concatenate([zeros, peer_keys[:, 1]], axis=-1)], axis=1)
    return dict(
        w_in=w_in_p, dn_conv_w=dn_conv_w, alog_row=lane_row(dn_a_log), dtb_row=lane_row(dn_dt_bias),
        dn_norm_g=row(dn_norm_g), lru_conv_w=lru_conv_w, lru_conv_b=row(lru_conv_b),
        lru_w_r=lru_w_r.astype(BF16), lru_b_r=row(lru_b_r), lru_w_i=lru_w_i.astype(BF16),
        lru_b_i=row(lru_b_i), lru_lambda=row(lru_lambda), w_branch_a=w_branch_a.astype(BF16),
        w_branch_b=w_branch_b.astype(BF16), w_out=w_out.astype(BF16), ln1_g=row(ln1_g),
        ln2_g=row(ln2_g), wq_t=wq_t, kblk=kblk, peer_u=peer_u.astype(BF16),
        peer_v=peer_v.astype(BF16), final_g=row(final_g))


def kernel(x_prompt, x_sample, state_dn, state_dn_conv, state_lru_h, state_lru_conv, w_in, dn_conv_w, dn_a_log, dn_dt_bias, dn_norm_g, lru_conv_w, lru_conv_b, lru_w_r, lru_b_r, lru_w_i, lru_b_i, lru_lambda, w_branch_a, w_branch_b, w_out, ln1_g, ln2_g, peer_w_q, peer_keys, peer_u, peer_v, final_g):
    depth = w_in.shape[0]
    assert depth == 1, "the final norm is fused into the layer's last kernel"
    bp = x_prompt.shape[0]
    dt = x_prompt.dtype
    p = _prepare(w_in[0], dn_conv_w[0], dn_a_log[0], dn_dt_bias[0], dn_norm_g[0], lru_conv_w[0],
                 lru_conv_b[0], lru_w_r[0], lru_b_r[0], lru_w_i[0], lru_b_i[0], lru_lambda[0],
                 w_branch_a[0], w_branch_b[0], w_out[0], ln1_g[0], ln2_g[0], peer_w_q[0],
                 peer_keys[0], peer_u[0], peer_v[0], final_g)
    prompt = _stream(
        x_prompt,
        jnp.zeros((bp, DN_HEADS, DN_DK, DN_DV), dt),
        jnp.zeros((bp, CONV_W - 1, DN_CONV_CH), dt),
        jnp.zeros((bp, LRU_WIDTH), dt),
        jnp.zeros((bp, CONV_W - 1, LRU_WIDTH), dt),
        p)
    sample = _stream(x_sample, state_dn[0], state_dn_conv[0], state_lru_h[0], state_lru_conv[0], p)
    y_p, s_p, db_p, h_p, lb_p = prompt
    y_s, s_s, db_s, h_s, lb_s = sample
    return (y_p, y_s, s_p[None], db_p[None], h_p[None], lb_p[None],
            s_s[None], db_s[None], h_s[None], lb_s[None])
```
